```python
import math
import jax, jax.numpy as jnp
from jax import lax
import numpy as np

D_MODEL = 2048
BATCH = 4
SEQ = 4096
DEPTH = 4

N_MEM = 256
RMS_EPS = 1e-5
CONV_WIDTH = 3
A_WIDTH = D_MODEL // 2
S5_WIDTH = D_MODEL // 2
S5_GROUP = 16
S5_GROUPS = S5_WIDTH // S5_GROUP
S5_STATE = 64
EVEN_IN = 3 * A_WIDTH + S5_WIDTH
HEAD_DIM = 64
N_Q_HEADS = D_MODEL // HEAD_DIM
Q_PER_KV = 8
N_KV_HEADS = N_Q_HEADS // Q_PER_KV
WINDOW = 128
BLOCK = 128
ODD_IN = (N_Q_HEADS + 2 * N_KV_HEADS) * HEAD_DIM
N_BUCKETS = 32
MAX_DISTANCE = 128
X_HEADS = 4
X_HEAD_DIM = D_MODEL // X_HEADS
D_FF = 5632
NEG_INF = -1e30
N_EVEN = (DEPTH + 1) // 2
N_ODD = DEPTH // 2

kernel_name = "hybrid_shortconv_s5_swa_sink_trunk"


def rms_norm(x, g):
    xf = x.astype(jnp.float32)
    y = xf * lax.rsqrt(jnp.mean(xf * xf, axis=-1, keepdims=True) + RMS_EPS)
    return (y * g.astype(jnp.float32)).astype(x.dtype)


def causal_dwconv(u, w):
    L = u.shape[1]
    up = jnp.pad(u, ((0, 0), (CONV_WIDTH - 1, 0), (0, 0)))
    y = up[:, 0:L] * w[0]
    for k in range(1, CONV_WIDTH):
        y = y + up[:, k:k + L] * w[k]
    return y


def t5_causal_bucket(rel):
    max_exact = N_BUCKETS // 2
    n = jnp.maximum(rel, 0)
    nf = jnp.maximum(n, max_exact).astype(jnp.float32)
    large = max_exact + (jnp.log(nf / max_exact) / math.log(MAX_DISTANCE / max_exact)
                         * (N_BUCKETS - max_exact)).astype(jnp.int32)
    large = jnp.minimum(large, N_BUCKETS - 1)
    return jnp.where(n < max_exact, n, large)


def s5_branch(u, a_re, a_im, log_dt, b_re, b_im, c_re, c_im, d, glu_w):
    f32 = jnp.float32
    Bsz, L, _ = u.shape
    ug = u.astype(f32).reshape(Bsz, L, S5_GROUPS, S5_GROUP)
    lam = lax.complex(a_re.astype(f32), a_im.astype(f32))
    dt = jnp.exp(log_dt.astype(f32))[:, None]
    a_bar = jnp.exp(lam * dt)
    b = lax.complex(b_re.astype(f32), b_im.astype(f32))
    b_bar = ((a_bar - 1.0) / lam)[..., None] * b
    bu = jnp.einsum('gph,blgh->blgp', b_bar, ug.astype(jnp.complex64))
    a_elems = jnp.broadcast_to(a_bar, (1, L) + a_bar.shape)

    def combine(e1, e2):
        a1, s1 = e1
        a2, s2 = e2
        return a1 * a2, a2 * s1 + s2

    _, states = lax.associative_scan(combine, (a_elems, bu), axis=1)
    c = lax.complex(c_re.astype(f32), c_im.astype(f32))
    y = jnp.real(jnp.einsum('ghp,blgp->blgh', c, states)) \
        + d.astype(f32).reshape(S5_GROUPS, S5_GROUP) * ug
    yg = jax.nn.gelu(y)
    gate = jnp.einsum('blgh,gho->blgo', yg, glu_w.astype(f32))
    out = yg * jax.nn.sigmoid(gate)
    return out.reshape(Bsz, L, S5_WIDTH).astype(u.dtype)


def conv_ssm_mixer(h, w_in, conv_w, a_re, a_im, log_dt, b_re, b_im, c_re, c_im, d, glu_w, w_out):
    z = h @ w_in
    gate_b, gate_c, xa, u = jnp.split(z, [A_WIDTH, 2 * A_WIDTH, 3 * A_WIDTH], axis=-1)
    ya = gate_b * causal_dwconv(gate_c * xa, conv_w)
    ys = s5_branch(u, a_re, a_im, log_dt, b_re, b_im, c_re, c_im, d, glu_w)
    return jnp.concatenate([ya, ys], axis=-1) @ w_out


def swa_sink_attention(h, w_qkv, b_qkv, sinks, rel_bias, w_out):
    Bsz, L, _ = h.shape
    nblk = L // BLOCK
    z = h @ w_qkv + b_qkv
    q, k, v = jnp.split(z, [N_Q_HEADS * HEAD_DIM, (N_Q_HEADS + N_KV_HEADS) * HEAD_DIM], axis=-1)
    q = q.reshape(Bsz, nblk, BLOCK, N_KV_HEADS, Q_PER_KV, HEAD_DIM)
    k = k.reshape(Bsz, nblk, BLOCK, N_KV_HEADS, HEAD_DIM)
    v = v.reshape(Bsz, nblk, BLOCK, N_KV_HEADS, HEAD_DIM)

    def with_prev(t):
        prev = jnp.pad(t, ((0, 0), (1, 0), (0, 0), (0, 0), (0, 0)))[:, :-1]
        return jnp.concatenate([prev, t], axis=2)

    kb, vb = with_prev(k), with_prev(v)
    s = jnp.einsum('bnqkgd,bnskd->bnkgqs', q, kb).astype(jnp.float32) * (HEAD_DIM ** -0.5)
    qi = jnp.arange(BLOCK, dtype=jnp.int32)[:, None]
    kj = jnp.arange(2 * BLOCK, dtype=jnp.int32)[None, :]
    rel = qi + BLOCK - kj
    bias = rel_bias.astype(jnp.float32)[t5_causal_bucket(rel)]
    bias = jnp.transpose(bias, (2, 0, 1)).reshape(N_KV_HEADS, Q_PER_KV, BLOCK, 2 * BLOCK)
    blk = jnp.arange(nblk, dtype=jnp.int32)[:, None, None]
    valid = (rel >= 0)[None] & (rel < WINDOW)[None] & (blk * BLOCK + kj[None] - BLOCK >= 0)
    s = jnp.where(valid[None, :, None, None], s + bias[None, None], NEG_INF)
    sink = jnp.broadcast_to(sinks.astype(jnp.float32).reshape(N_KV_HEADS, Q_PER_KV)[None, None, :, :, None, None],
                            s.shape[:-1] + (1,))
    p = jax.nn.softmax(jnp.concatenate([s, sink], axis=-1), axis=-1)[..., :-1]
    o = jnp.einsum('bnkgqs,bnskd->bnqkgd', p.astype(vb.dtype), vb)
    return o.reshape(Bsz, L, N_Q_HEADS * HEAD_DIM) @ w_out


def memory_cross_attention(h, mem_n, w_q, w_kv, w_o):
    Bsz, L, _ = h.shape
    q = (h @ w_q).reshape(Bsz, L, X_HEADS, X_HEAD_DIM)
    k, v = jnp.split(mem_n @ w_kv, 2, axis=-1)
    k = k.reshape(Bsz, -1, X_HEADS, X_HEAD_DIM)
    v = v.reshape(Bsz, -1, X_HEADS, X_HEAD_DIM)
    s = jnp.einsum('blhd,bmhd->bhlm', q, k).astype(jnp.float32) * (X_HEAD_DIM ** -0.5)
    p = jax.nn.softmax(s, axis=-1).astype(v.dtype)
    o = jnp.einsum('bhlm,bmhd->blhd', p, v).reshape(Bsz, L, D_MODEL)
    return o @ w_o


def conv_gated_mlp(h, w_gate, w_up, conv_w, conv_b, w_down):
    g = causal_dwconv(h @ w_gate, conv_w) + conv_b
    return (jax.nn.silu(g) * (h @ w_up)) @ w_down


def setup_inputs(seed: int = 0) -> dict:
    key = jax.random.key(seed)
    ks = jax.random.split(key, 40)
    f32 = jnp.float32
    D = D_MODEL

    def nrm(k, shape, fan_in):
        return jax.random.normal(k, shape, f32) * (fan_in ** -0.5)

    def gain(k, shape):
        return 1.0 + 0.02 * jax.random.normal(k, shape, f32)

    a_re = -0.5 * jnp.exp(0.05 * jax.random.normal(ks[8], (N_EVEN, S5_GROUPS, S5_STATE), f32))
    a_im = math.pi * jnp.arange(S5_STATE, dtype=f32) + 0.01 * jax.random.normal(ks[9], (N_EVEN, S5_GROUPS, S5_STATE), f32)
    log_dt = jax.random.uniform(ks[10], (N_EVEN, S5_GROUPS), f32, math.log(1e-3), math.log(1e-1))
    return {
        "x": jax.random.normal(ks[0], (BATCH, SEQ, D), f32),
        "mem": jax.random.normal(ks[1], (BATCH, N_MEM, D), f32),
        "norm_mix": gain(ks[2], (DEPTH, D)),
        "norm_xattn": gain(ks[3], (DEPTH, D)),
        "norm_ffn": gain(ks[4], (DEPTH, D)),
        "norm_final": gain(ks[5], (D,)),
        "norm_mem": gain(ks[6], (D,)),
        "rel_bias": 0.5 * jax.random.normal(ks[7], (N_BUCKETS, N_Q_HEADS), f32),
        "ev_w_in": nrm(ks[11], (N_EVEN, D, EVEN_IN), D),
        "ev_conv_w": nrm(ks[12], (N_EVEN, CONV_WIDTH, A_WIDTH), CONV_WIDTH),
        "s5_a_re": a_re,
        "s5_a_im": a_im,
        "s5_log_dt": log_dt,
        "s5_b_re": nrm(ks[13], (N_EVEN, S5_GROUPS, S5_STATE, S5_GROUP), 2 * S5_GROUP),
        "s5_b_im": nrm(ks[14], (N_EVEN, S5_GROUPS, S5_STATE, S5_GROUP), 2 * S5_GROUP),
        "s5_c_re": nrm(ks[15], (N_EVEN, S5_GROUPS, S5_GROUP, S5_STATE), 2 * S5_STATE),
        "s5_c_im": nrm(ks[16], (N_EVEN, S5_GROUPS, S5_GROUP, S5_STATE), 2 * S5_STATE),
        "s5_d": jax.random.normal(ks[17], (N_EVEN, S5_WIDTH), f32),
        "s5_glu_w": nrm(ks[18], (N_EVEN, S5_GROUPS, S5_GROUP, S5_GROUP), S5_GROUP),
        "ev_w_out": nrm(ks[19], (N_EVEN, D, D), D),
        "od_w_qkv": nrm(ks[20], (N_ODD, D, ODD_IN), D),
        "od_b_qkv": 0.02 * jax.random.normal(ks[21], (N_ODD, ODD_IN), f32),
        "od_sinks": jax.random.normal(ks[22], (N_ODD, N_Q_HEADS), f32),
        "od_w_out": nrm(ks[23], (N_ODD, N_Q_HEADS * HEAD_DIM, D), D),
        "xa_w_q": nrm(ks[24], (DEPTH, D, D), D),
        "xa_w_kv": nrm(ks[25], (DEPTH, D, 2 * D), D),
        "xa_w_o": nrm(ks[26], (DEPTH, D, D), D),
        "ff_w_gate": nrm(ks[27], (DEPTH, D, D_FF), D),
        "ff_w_up": nrm(ks[28], (DEPTH, D, D_FF), D),
        "ff_conv_w": nrm(ks[29], (DEPTH, CONV_WIDTH, D_FF), CONV_WIDTH),
        "ff_conv_b": 0.02 * jax.random.normal(ks[30], (DEPTH, D_FF), f32),
        "ff_w_down": nrm(ks[31], (DEPTH, D_FF, D), D_FF),
    }


def reference(x, mem, norm_mix, norm_xattn, norm_ffn, norm_final, norm_mem, rel_bias,
              ev_w_in, ev_conv_w, s5_a_re, s5_a_im, s5_log_dt, s5_b_re, s5_b_im,
              s5_c_re, s5_c_im, s5_d, s5_glu_w, ev_w_out,
              od_w_qkv, od_b_qkv, od_sinks, od_w_out,
              xa_w_q, xa_w_kv, xa_w_o,
              ff_w_gate, ff_w_up, ff_conv_w, ff_conv_b, ff_w_down):
    mem_n = rms_norm(mem, norm_mem)
    h = x
    for l in range(DEPTH):
        i = l // 2
        hn = rms_norm(h, norm_mix[l])
        if l % 2 == 0:
            h = h + conv_ssm_mixer(hn, ev_w_in[i], ev_conv_w[i], s5_a_re[i], s5_a_im[i],
                                   s5_log_dt[i], s5_b_re[i], s5_b_im[i], s5_c_re[i],
                                   s5_c_im[i], s5_d[i], s5_glu_w[i], ev_w_out[i])
        else:
            h = h + swa_sink_attention(hn, od_w_qkv[i], od_b_qkv[i], od_sinks[i], rel_bias, od_w_out[i])
        h = h + memory_cross_attention(rms_norm(h, norm_xattn[l]), mem_n, xa_w_q[l], xa_w_kv[l], xa_w_o[l])
        h = h + conv_gated_mlp(rms_norm(h, norm_ffn[l]), ff_w_gate[l], ff_w_up[l],
                               ff_conv_w[l], ff_conv_b[l], ff_w_down[l])
    return rms_norm(h, norm_final)
```

```python
import functools
import math

import numpy as np
import jax
import jax.numpy as jnp
from jax import lax
from jax.experimental import pallas as pl
from jax.experimental.pallas import tpu as pltpu

F32 = jnp.float32
BF16 = jnp.bfloat16

RMS_EPS = 1e-5
NEG_INF = -1e30

HEAD_DIM = 64
Q_PER_KV = 8
N_KV_HEADS = 4
N_Q_HEADS = Q_PER_KV * N_KV_HEADS
ATT_BLOCK = 128
N_BUCKETS = 32
MAX_DISTANCE = 128
X_HEADS = 4
S5_GROUP = 16
S5_STATE = 64

LANES = 128
SUBLANES = 8
VMEM_LIMIT_BYTES = 60000 * 1024

S5_CHUNK = 16
GROUPS_PER_TILE = LANES // S5_GROUP


def _params(*sem):
    return pltpu.CompilerParams(dimension_semantics=sem, vmem_limit_bytes=VMEM_LIMIT_BYTES)


def _rms(x, g):
    ms = jnp.mean(x * x, axis=-1, keepdims=True)
    return x * lax.rsqrt(ms + RMS_EPS) * g


def _rmsnorm_kernel(x_ref, g_ref, o_ref):
    o_ref[...] = _rms(x_ref[...], g_ref[...]).astype(o_ref.dtype)


def _rmsnorm(x, g, *, tm=512):
    M, D = x.shape
    tm = min(tm, M)
    return pl.pallas_call(
        _rmsnorm_kernel,
        grid=(M // tm,),
        in_specs=[pl.BlockSpec((tm, D), lambda i: (i, 0)), pl.BlockSpec((1, D), lambda i: (0, 0))],
        out_specs=pl.BlockSpec((tm, D), lambda i: (i, 0)),
        out_shape=jax.ShapeDtypeStruct((M, D), BF16),
        compiler_params=_params("arbitrary"),
        name="rmsnorm",
    )(x, g.reshape(1, D))


def _mm_kernel(has_bias, a_ref, w_ref, *rest):
    if has_bias:
        b_ref, o_ref, wbf_ref = rest
    else:
        o_ref, wbf_ref = rest

    @pl.when(pl.program_id(1) == 0)
    def _():
        wbf_ref[...] = w_ref[...].astype(BF16)

    acc = jnp.dot(a_ref[...], wbf_ref[...], preferred_element_type=F32)
    if has_bias:
        acc = acc + b_ref[...]
    o_ref[...] = acc.astype(o_ref.dtype)


def _mm(a, w, *, col0=0, n=None, bias=None, out_dtype=BF16, tm=1024, tn=1024, name="mm"):
    M, K = a.shape
    n = w.shape[1] - col0 if n is None else n
    tm, tn = min(tm, M), min(tn, n)
    assert M % tm == 0 and n % tn == 0 and col0 % tn == 0
    cb = col0 // tn
    in_specs = [pl.BlockSpec((tm, K), lambda j, i: (i, 0)),
                pl.BlockSpec((K, tn), lambda j, i: (0, cb + j))]
    args = [a, w]
    if bias is not None:
        in_specs.append(pl.BlockSpec((1, tn), lambda j, i: (0, cb + j)))
        args.append(bias.reshape(1, -1))
    return pl.pallas_call(
        functools.partial(_mm_kernel, bias is not None),
        grid=(n // tn, M // tm),
        in_specs=in_specs,
        out_specs=pl.BlockSpec((tm, tn), lambda j, i: (i, j)),
        out_shape=jax.ShapeDtypeStruct((M, n), out_dtype),
        scratch_shapes=[pltpu.VMEM((K, tn), BF16)],
        compiler_params=_params("arbitrary", "arbitrary"),
        name=name,
    )(*args)


def _mm_res_norm_kernel(emit_h, a_ref, w_ref, h_ref, g_ref, *rest):
    if emit_h:
        acc_ref, hn_ref = rest
    else:
        hn_ref, acc_ref = rest
    k = pl.program_id(1)

    @pl.when(k == 0)
    def _():
        acc_ref[...] = h_ref[...]

    acc_ref[...] += jnp.dot(a_ref[...], w_ref[...], preferred_element_type=F32)

    @pl.when(k == pl.num_programs(1) - 1)
    def _():
        hn_ref[...] = _rms(acc_ref[...], g_ref[...]).astype(hn_ref.dtype)


def _mm_res_norm(a, w, h, g, *, emit_h=True, norm_dtype=BF16, tm=512, tk=512, name="mm_res_norm"):
    M, K = a.shape
    D = w.shape[1]
    tm, tk = min(tm, M), min(tk, K)
    assert M % tm == 0 and K % tk == 0
    row = pl.BlockSpec((tm, D), lambda i, k: (i, 0))
    out_shape = [jax.ShapeDtypeStruct((M, D), norm_dtype)]
    out_specs = [row]
    scratch = []
    if emit_h:
        out_shape.insert(0, jax.ShapeDtypeStruct((M, D), F32))
        out_specs.insert(0, row)
    else:
        scratch.append(pltpu.VMEM((tm, D), F32))
    res = pl.pallas_call(
        functools.partial(_mm_res_norm_kernel, emit_h),
        grid=(M // tm, K // tk),
        in_specs=[pl.BlockSpec((tm, tk), lambda i, k: (i, k)),
                  pl.BlockSpec((tk, D), lambda i, k: (k, 0)),
                  row,
                  pl.BlockSpec((1, D), lambda i, k: (0, 0))],
        out_specs=out_specs,
        out_shape=out_shape,
        scratch_shapes=scratch,
        compiler_params=_params("arbitrary", "arbitrary"),
        name=name,
    )(a, w, h, g.reshape(1, D))
    return res if emit_h else (None, res[0])


def _conv3_rows(p, prev8, w_ref):
    w0, w1, w2 = w_ref[0:1, :], w_ref[1:2, :], w_ref[2:3, :]
    full = w2 * p + w1 * pltpu.roll(p, 1, 0) + w0 * pltpu.roll(p, 2, 0)
    e = jnp.concatenate([prev8, p[0:SUBLANES]], axis=0)
    f = w2 * e + w1 * pltpu.roll(e, 1, 0) + w0 * pltpu.roll(e, 2, 0)
    return full, f[SUBLANES:2 * SUBLANES]


def _load_halo(halo_ref, tiles_per_seq):
    @pl.when(pl.program_id(1) % tiles_per_seq == 0)
    def _():
        halo_ref[...] = jnp.zeros_like(halo_ref)
    return halo_ref[...]


def _mixer_a_kernel(tiles_per_seq, a_ref, wb_ref, wc_ref, wx_ref, cw_ref, o_ref,
                    wb_s, wc_s, wx_s, halo_s):
    @pl.when(pl.program_id(1) == 0)
    def _():
        wb_s[...] = wb_ref[...].astype(BF16)
        wc_s[...] = wc_ref[...].astype(BF16)
        wx_s[...] = wx_ref[...].astype(BF16)

    a = a_ref[...]
    tm = a.shape[0]
    gb = jnp.dot(a, wb_s[...], preferred_element_type=F32)
    p = (jnp.dot(a, wc_s[...], preferred_element_type=F32)
         * jnp.dot(a, wx_s[...], preferred_element_type=F32))
    prev8 = _load_halo(halo_s, tiles_per_seq)
    full, first8 = _conv3_rows(p, prev8, cw_ref)
    halo_s[...] = p[tm - SUBLANES:tm]
    o_ref[...] = (gb * full).astype(o_ref.dtype)
    o_ref[0:SUBLANES, :] = (gb[0:SUBLANES] * first8).astype(o_ref.dtype)


def _mixer_a(hn, w_in, conv_w, seq_len, *, tm=1024, tn=512):
    M, K = hn.shape
    ca = conv_w.shape[1]
    tm, tn = min(tm, seq_len), min(tn, ca)
    assert seq_len % tm == 0 and ca % tn == 0
    nj = ca // tn
    wspec = lambda off: pl.BlockSpec((K, tn), lambda j, i: (0, off * nj + j))
    return pl.pallas_call(
        functools.partial(_mixer_a_kernel, seq_len // tm),
        grid=(nj, M // tm),
        in_specs=[pl.BlockSpec((tm, K), lambda j, i: (i, 0)),
                  wspec(0), wspec(1), wspec(2),
                  pl.BlockSpec((3, tn), lambda j, i: (0, j))],
        out_specs=pl.BlockSpec((tm, tn), lambda j, i: (i, j)),
        out_shape=jax.ShapeDtypeStruct((M, ca), BF16),
        scratch_shapes=[pltpu.VMEM((K, tn), BF16)] * 3 + [pltpu.VMEM((SUBLANES, tn), F32)],
        compiler_params=_params("arbitrary", "arbitrary"),
        name="mixer_a",
    )(hn, w_in, w_in, w_in, conv_w)


def _ffn_up_kernel(tiles_per_seq, a_ref, wg_ref, wu_ref, cw_ref, cb_ref, o_ref, wg_s, wu_s, halo_s):
    @pl.when(pl.program_id(1) == 0)
    def _():
        wg_s[...] = wg_ref[...].astype(BF16)
        wu_s[...] = wu_ref[...].astype(BF16)

    a = a_ref[...]
    tm = a.shape[0]
    gate = jnp.dot(a, wg_s[...], preferred_element_type=F32)
    up = jnp.dot(a, wu_s[...], preferred_element_type=F32)
    prev8 = _load_halo(halo_s, tiles_per_seq)
    full, first8 = _conv3_rows(gate, prev8, cw_ref)
    halo_s[...] = gate[tm - SUBLANES:tm]
    b = cb_ref[...]
    o_ref[...] = (jax.nn.silu(full + b) * up).astype(o_ref.dtype)
    o_ref[0:SUBLANES, :] = (jax.nn.silu(first8 + b) * up[0:SUBLANES]).astype(o_ref.dtype)


def _ffn_up(hn, w_gate, w_up, conv_w, conv_b, seq_len, *, tm=1024, tn=512):
    M, K = hn.shape
    dff = w_gate.shape[1]
    tm, tn = min(tm, seq_len), min(tn, dff)
    assert seq_len % tm == 0 and dff % tn == 0
    wspec = pl.BlockSpec((K, tn), lambda j, i: (0, j))
    return pl.pallas_call(
        functools.partial(_ffn_up_kernel, seq_len // tm),
        grid=(dff // tn, M // tm),
        in_specs=[pl.BlockSpec((tm, K), lambda j, i: (i, 0)),
                  wspec, wspec,
                  pl.BlockSpec((3, tn), lambda j, i: (0, j)),
                  pl.BlockSpec((1, tn), lambda j, i: (0, j))],
        out_specs=pl.BlockSpec((tm, tn), lambda j, i: (i, j)),
        out_shape=jax.ShapeDtypeStruct((M, dff), BF16),
        scratch_shapes=[pltpu.VMEM((K, tn), BF16)] * 2 + [pltpu.VMEM((SUBLANES, tn), F32)],
        compiler_params=_params("arbitrary", "arbitrary"),
        name="ffn_up",
    )(hn, w_gate, w_up, conv_w, conv_b.reshape(1, dff))


def _s5_tables(a_re, a_im, log_dt, b_re, b_im, c_re, c_im, glu_w):
    hp = lax.Precision.HIGHEST
    T, gt = S5_CHUNK, GROUPS_PER_TILE
    G, P = a_re.shape
    H = b_re.shape[-1]
    J = G // gt
    dt = jnp.exp(log_dt)[:, None]
    lr, li = a_re * dt, a_im * dt

    def a_pow(taus):
        tau = jnp.asarray(np.asarray(taus, np.float32))[:, None, None]
        mag = jnp.exp(tau * lr)
        return mag * jnp.cos(tau * li), mag * jnp.sin(tau * li)

    pw_re, pw_im = a_pow(np.arange(T + 1))
    nr, ni = pw_re[1] - 1.0, pw_im[1]
    den = a_re * a_re + a_im * a_im
    qr, qi = (nr * a_re + ni * a_im) / den, (ni * a_re - nr * a_im) / den
    bb_re = qr[..., None] * b_re - qi[..., None] * b_im
    bb_im = qr[..., None] * b_im + qi[..., None] * b_re
    cp_re = c_re[None] * pw_re[:, :, None, :] - c_im[None] * pw_im[:, :, None, :]
    cp_im = c_re[None] * pw_im[:, :, None, :] + c_im[None] * pw_re[:, :, None, :]
    kern = (jnp.einsum('tgop,gph->tgoh', cp_re[:T], bb_re, precision=hp)
            - jnp.einsum('tgop,gph->tgoh', cp_im[:T], bb_im, precision=hp))
    eye = jnp.eye(gt, dtype=F32)
    s_idx = np.arange(T)[:, None]
    t_idx = np.arange(T)[None, :]
    toe = jnp.where((t_idx >= s_idx)[:, :, None, None, None],
                    kern[np.maximum(t_idx - s_idx, 0)], 0.0)
    toe = toe.reshape(T, T, J, gt, H, H)
    mi = jnp.einsum('stjaoh,ab->jsahtbo', toe, eye).reshape(J, T * LANES, T * LANES)
    rev_re, rev_im = a_pow(T - 1 - np.arange(T))
    wv_re = rev_re[..., None] * bb_re[None] - rev_im[..., None] * bb_im[None]
    wv_im = rev_re[..., None] * bb_im[None] + rev_im[..., None] * bb_re[None]

    def in_map(x):
        x = x.reshape(T, J, gt, P, H)
        return jnp.einsum('sjaph,ab->jsahbp', x, eye).reshape(J, T * LANES, gt * P)

    wv = jnp.concatenate([in_map(wv_re), in_map(wv_im)], axis=-1)

    def out_map(x):
        x = x.reshape(T, J, gt, H, P)
        return jnp.einsum('tjaop,ab->japtbo', x, eye).reshape(J, gt * P, T * LANES)

    wo = jnp.concatenate([out_map(cp_re[1:]), out_map(-cp_im[1:])], axis=1)
    a_chunk = jnp.concatenate([pw_re[T].reshape(J, 1, gt * P), pw_im[T].reshape(J, 1, gt * P)], axis=-1)
    glu = jnp.einsum('jaho,ab->jahbo', glu_w.reshape(J, gt, H, H), eye).reshape(J, LANES, LANES)
    return mi.astype(BF16), wv.astype(BF16), wo.astype(BF16), a_chunk, glu.astype(BF16)


def _gelu_tanh(x):
    c = math.sqrt(2.0 / math.pi)
    return 0.5 * x * (1.0 + jnp.tanh(c * (x + 0.044715 * (x * x * x))))


def _s5_kernel(u_ref, mi_ref, wv_ref, wo_ref, ac_ref, d_ref, glu_ref, o_ref,
               ubf_s, v_s, sp_s, y_s, ynat_s):
    T = S5_CHUNK
    nc = u_ref.shape[0] // T
    ns = ac_ref.shape[-1] // 2
    for t in range(T):
        ubf_s[:, t * LANES:(t + 1) * LANES] = u_ref[pl.ds(t, nc, stride=T), :].astype(BF16)
    u_chunks = ubf_s[...]
    v_s[...] = jnp.dot(u_chunks, wv_ref[0], preferred_element_type=F32)
    ar, ai = ac_ref[0, :, 0:ns], ac_ref[0, :, ns:2 * ns]

    def step(c, carry):
        sr, si = carry
        sp_s[pl.ds(c, 1), 0:ns] = sr
        sp_s[pl.ds(c, 1), ns:2 * ns] = si
        vr = v_s[pl.ds(c, 1), 0:ns]
        vi = v_s[pl.ds(c, 1), ns:2 * ns]
        return ar * sr - ai * si + vr, ar * si + ai * sr + vi

    zero = jnp.zeros((1, ns), F32)
    lax.fori_loop(0, nc, step, (zero, zero))
    y_s[...] = (jnp.dot(u_chunks, mi_ref[0], preferred_element_type=F32)
                + jnp.dot(sp_s[...].astype(BF16), wo_ref[0], preferred_element_type=F32))
    for t in range(T):
        ynat_s[pl.ds(t, nc, stride=T), :] = y_s[:, t * LANES:(t + 1) * LANES]
    y = ynat_s[...] + d_ref[...] * u_ref[...]
    yg = _gelu_tanh(y)
    gate = jnp.dot(yg.astype(BF16), glu_ref[0], preferred_element_type=F32)
    o_ref[...] = (yg * jax.nn.sigmoid(gate)).astype(o_ref.dtype)


def _s5(u, tables, d, batch, seq_len):
    mi, wv, wo, a_chunk, glu = tables
    J = mi.shape[0]
    M, width = u.shape
    T = S5_CHUNK
    nc = seq_len // T
    n2 = wv.shape[-1]
    return pl.pallas_call(
        _s5_kernel,
        grid=(J, batch),
        in_specs=[pl.BlockSpec((seq_len, LANES), lambda j, b: (b, j)),
                  pl.BlockSpec((1, T * LANES, T * LANES), lambda j, b: (j, 0, 0)),
                  pl.BlockSpec((1, T * LANES, n2), lambda j, b: (j, 0, 0)),
                  pl.BlockSpec((1, n2, T * LANES), lambda j, b: (j, 0, 0)),
                  pl.BlockSpec((1, 1, n2), lambda j, b: (j, 0, 0)),
                  pl.BlockSpec((1, LANES), lambda j, b: (0, j)),
                  pl.BlockSpec((1, LANES, LANES), lambda j, b: (j, 0, 0))],
        out_specs=pl.BlockSpec((seq_len, LANES), lambda j, b: (b, j)),
        out_shape=jax.ShapeDtypeStruct((M, width), BF16),
        scratch_shapes=[pltpu.VMEM((nc, T * LANES), BF16),
                        pltpu.VMEM((nc, n2), F32),
                        pltpu.VMEM((nc, n2), F32),
                        pltpu.VMEM((nc, T * LANES), F32),
                        pltpu.VMEM((seq_len, LANES), F32)],
        compiler_params=_params("arbitrary", "arbitrary"),
        name="s5",
    )(u, mi, wv, wo, a_chunk, d.reshape(1, width), glu)


def _t5_bucket_np(rel):
    max_exact = N_BUCKETS // 2
    n = np.maximum(rel, 0)
    nf = np.maximum(n, max_exact).astype(np.float32)
    large = max_exact + (np.log(nf / np.float32(max_exact)) / np.float32(math.log(MAX_DISTANCE / max_exact))
                         * np.float32(N_BUCKETS - max_exact)).astype(np.int32)
    large = np.minimum(large, N_BUCKETS - 1)
    return np.where(n < max_exact, n, large).astype(np.int32)


def _swa_table_kernel(bias_ref, sink_ref, bucket_ref, valid_ref, add_ref, mul_ref):
    h = pl.program_id(1)
    bucket = bucket_ref[...]
    valid = valid_ref[0] != 0
    acc = jnp.full(bucket.shape, NEG_INF, F32)
    for b in range(N_BUCKETS):
        acc = jnp.where(bucket == b, bias_ref[b, h], acc)
    acc = jnp.where(valid, acc, NEG_INF)
    col = lax.broadcasted_iota(jnp.int32, bucket.shape, 1)
    add_ref[0, 0] = jnp.where(col == 0, sink_ref[h], acc)
    mul_ref[0, 0] = jnp.where(valid, HEAD_DIM ** -0.5, 0.0).astype(F32)


def _swa_tables(rel_bias, sinks):
    blk = ATT_BLOCK
    qi = np.arange(blk)[:, None]
    kj = np.arange(2 * blk)[None, :]
    rel = qi + blk - kj
    in_window = (rel >= 0) & (rel < blk)
    valid = np.stack([in_window & (kj >= blk), in_window]).astype(np.int32)
    bucket = _t5_bucket_np(rel)
    shape = jax.ShapeDtypeStruct((2, N_Q_HEADS, blk, 2 * blk), F32)
    tab = pl.BlockSpec((1, 1, blk, 2 * blk), lambda v, h: (v, h, 0, 0))
    return pl.pallas_call(
        _swa_table_kernel,
        grid=(2, N_Q_HEADS),
        in_specs=[pl.BlockSpec(memory_space=pltpu.SMEM),
                  pl.BlockSpec(memory_space=pltpu.SMEM),
                  pl.BlockSpec((blk, 2 * blk), lambda v, h: (0, 0)),
                  pl.BlockSpec((1, blk, 2 * blk), lambda v, h: (v, 0, 0))],
        out_specs=[tab, tab],
        out_shape=[shape, shape],
        compiler_params=_params("arbitrary", "arbitrary"),
        name="swa_tables",
    )(rel_bias, sinks, jnp.asarray(bucket), jnp.asarray(valid))


def _swa_kernel(q_ref, kp_ref, kc_ref, vp_ref, vc_ref, add_ref, mul_ref, o_ref):
    blk, hd, g = ATT_BLOCK, HEAD_DIM, Q_PER_KV
    first_row = lax.broadcasted_iota(jnp.int32, (2 * blk, hd), 0) == 0
    outs = []
    for k in range(N_KV_HEADS):
        sl = slice(k * hd, (k + 1) * hd)
        keys = jnp.concatenate([kp_ref[:, sl], kc_ref[:, sl]], axis=0)
        vals = jnp.concatenate([vp_ref[:, sl], vc_ref[:, sl]], axis=0)
        vals = jnp.where(first_row, jnp.zeros_like(vals), vals)
        q = jnp.concatenate([q_ref[:, (k * g + i) * hd:(k * g + i + 1) * hd] for i in range(g)],
                            axis=0)
        s = lax.dot_general(q, keys, (((1,), (1,)), ((), ())), preferred_element_type=F32)
        s = (s * mul_ref[0, k * g:(k + 1) * g].reshape(g * blk, 2 * blk)
             + add_ref[0, k * g:(k + 1) * g].reshape(g * blk, 2 * blk))
        m = jnp.max(s, axis=-1, keepdims=True)
        p = jnp.exp(s - m)
        den = jnp.sum(p, axis=-1, keepdims=True)
        o = jnp.dot(p.astype(BF16), vals, preferred_element_type=F32) / den
        outs.extend(o[i * blk:(i + 1) * blk] for i in range(g))
    o_ref[...] = jnp.concatenate(outs, axis=-1).astype(o_ref.dtype)


def _swa(z, tables, batch, seq_len):
    add, mul = tables
    blk = ATT_BLOCK
    nblk = seq_len // blk
    M = z.shape[0]
    dq = N_Q_HEADS * HEAD_DIM
    dkv = N_KV_HEADS * HEAD_DIM
    kcol, vcol = dq // dkv, dq // dkv + 1
    cur = lambda c: pl.BlockSpec((blk, dkv), lambda b, n: (b * nblk + n, c))
    prev = lambda c: pl.BlockSpec((blk, dkv), lambda b, n: (b * nblk + jnp.maximum(n - 1, 0), c))
    tab = pl.BlockSpec((1, N_Q_HEADS, blk, 2 * blk), lambda b, n: (jnp.minimum(n, 1), 0, 0, 0))
    return pl.pallas_call(
        _swa_kernel,
        grid=(batch, nblk),
        in_specs=[pl.BlockSpec((blk, dq), lambda b, n: (b * nblk + n, 0)),
                  prev(kcol), cur(kcol), prev(vcol), cur(vcol), tab, tab],
        out_specs=pl.BlockSpec((blk, dq), lambda b, n: (b * nblk + n, 0)),
        out_shape=jax.ShapeDtypeStruct((M, dq), BF16),
        compiler_params=_params("arbitrary", "arbitrary"),
        name="swa",
    )(z, z, z, z, z, add, mul)


def _xattn_kernel(q_ref, kv_ref, o_ref):
    d = q_ref.shape[1]
    hd = d // X_HEADS
    outs = []
    for h in range(X_HEADS):
        q = q_ref[:, h * hd:(h + 1) * hd]
        k = kv_ref[:, h * hd:(h + 1) * hd]
        v = kv_ref[:, d + h * hd:d + (h + 1) * hd]
        s = lax.dot_general(q, k, (((1,), (1,)), ((), ())), preferred_element_type=F32) * (hd ** -0.5)
        m = jnp.max(s, axis=-1, keepdims=True)
        p = jnp.exp(s - m)
        den = jnp.sum(p, axis=-1, keepdims=True)
        outs.append(jnp.dot(p.astype(BF16), v, preferred_element_type=F32) / den)
    o_ref[...] = jnp.concatenate(outs, axis=-1).astype(o_ref.dtype)


def _xattn(q, kv, batch, seq_len, n_mem, *, tq=512):
    M, d = q.shape
    tq = min(tq, seq_len)
    nq = seq_len // tq
    return pl.pallas_call(
        _xattn_kernel,
        grid=(batch, nq),
        in_specs=[pl.BlockSpec((tq, d), lambda b, i: (b * nq + i, 0)),
                  pl.BlockSpec((n_mem, 2 * d), lambda b, i: (b, 0))],
        out_specs=pl.BlockSpec((tq, d), lambda b, i: (b * nq + i, 0)),
        out_shape=jax.ShapeDtypeStruct((M, d), BF16),
        compiler_params=_params("arbitrary", "arbitrary"),
        name="xattn",
    )(q, kv)


def kernel(x, mem, norm_mix, norm_xattn, norm_ffn, norm_final, norm_mem, rel_bias, ev_w_in, ev_conv_w, s5_a_re, s5_a_im, s5_log_dt, s5_b_re, s5_b_im, s5_c_re, s5_c_im, s5_d, s5_glu_w, ev_w_out, od_w_qkv, od_b_qkv, od_sinks, od_w_out, xa_w_q, xa_w_kv, xa_w_o, ff_w_gate, ff_w_up, ff_conv_w, ff_conv_b, ff_w_down):
    batch, seq_len, d = x.shape
    n_mem = mem.shape[1]
    depth = norm_mix.shape[0]
    M = batch * seq_len
    a_width = ev_conv_w.shape[-1]

    mem_n = _rmsnorm(mem.reshape(batch * n_mem, d), norm_mem)
    h = x.reshape(M, d)
    hn = _rmsnorm(h, norm_mix[0])
    for l in range(depth):
        i = l // 2
        if l % 2 == 0:
            ya = _mixer_a(hn, ev_w_in[i], ev_conv_w[i], seq_len)
            u = _mm(hn, ev_w_in[i], col0=3 * a_width, out_dtype=F32, name="s5_in")
            tables = _s5_tables(s5_a_re[i], s5_a_im[i], s5_log_dt[i], s5_b_re[i], s5_b_im[i],
                                s5_c_re[i], s5_c_im[i], s5_glu_w[i])
            ys = _s5(u, tables, s5_d[i], batch, seq_len)
            mixed = jnp.concatenate([ya, ys], axis=-1)
            h, hn = _mm_res_norm(mixed, ev_w_out[i].astype(BF16), h, norm_xattn[l], name="ev_out")
        else:
            z = _mm(hn, od_w_qkv[i], bias=od_b_qkv[i], tn=1280, name="qkv")
            o = _swa(z, _swa_tables(rel_bias, od_sinks[i]), batch, seq_len)
            h, hn = _mm_res_norm(o, od_w_out[i].astype(BF16), h, norm_xattn[l], name="od_out")
        q = _mm(hn, xa_w_q[l], name="xa_q")
        kv = _mm(mem_n, xa_w_kv[l], name="xa_kv")
        o = _xattn(q, kv, batch, seq_len, n_mem)
        h, hn = _mm_res_norm(o, xa_w_o[l].astype(BF16), h, norm_ffn[l], name="xa_out")
        act = _ffn_up(hn, ff_w_gate[l], ff_w_up[l], ff_conv_w[l], ff_conv_b[l], seq_len)
        if l + 1 < depth:
            h, hn = _mm_res_norm(act, ff_w_down[l].astype(BF16), h, norm_mix[l + 1], name="ff_down")
        else:
            _, out = _mm_res_norm(act, ff_w_down[l].astype(BF16), h, norm_final,
                                  emit_h=False, norm_dtype=F32, name="ff_down_final")
    return out.reshape(batch, seq_len, d)
```

```python
import functools
import math

import numpy as np
import jax
import jax.numpy as jnp
from jax import lax
from jax.experimental import pallas as pl
from jax.experimental.pallas import tpu as pltpu

F32 = jnp.float32
BF16 = jnp.bfloat16

RMS_EPS = 1e-5
NEG_INF = -1e30

HEAD_DIM = 64
Q_PER_KV = 8
N_KV_HEADS = 4
N_Q_HEADS = Q_PER_KV * N_KV_HEADS
ATT_BLOCK = 128
N_BUCKETS = 32
MAX_DISTANCE = 128
X_HEADS = 4
S5_GROUP = 16
S5_STATE = 64

LANES = 128
SUBLANES = 8
MXU_DIM = 256
VMEM_LIMIT_BYTES = 60000 * 1024

S5_CHUNK = 16
GROUPS_PER_TILE = LANES // S5_GROUP

ROW_CHUNK = 256


def _params(*sem):
    return pltpu.CompilerParams(dimension_semantics=sem, vmem_limit_bytes=VMEM_LIMIT_BYTES)


def _rms(x, g):
    ms = jnp.mean(x * x, axis=-1, keepdims=True)
    return x * lax.rsqrt(ms + RMS_EPS) * g


def _rmsnorm_kernel(x_ref, g_ref, o_ref):
    o_ref[...] = _rms(x_ref[...], g_ref[...]).astype(o_ref.dtype)


def _rmsnorm(x, g, *, tm=512):
    M, D = x.shape
    tm = min(tm, M)
    return pl.pallas_call(
        _rmsnorm_kernel,
        grid=(M // tm,),
        in_specs=[pl.BlockSpec((tm, D), lambda i: (i, 0)), pl.BlockSpec((1, D), lambda i: (0, 0))],
        out_specs=pl.BlockSpec((tm, D), lambda i: (i, 0)),
        out_shape=jax.ShapeDtypeStruct((M, D), BF16),
        compiler_params=_params("arbitrary"),
        name="rmsnorm",
    )(x, g.reshape(1, D))


def _mm_kernel(has_bias, a_ref, w_ref, *rest):
    if has_bias:
        b_ref, o_ref, wbf_ref = rest
    else:
        o_ref, wbf_ref = rest

    @pl.when(pl.program_id(1) == 0)
    def _():
        wbf_ref[...] = w_ref[...].astype(BF16)

    acc = jnp.dot(a_ref[...], wbf_ref[...], preferred_element_type=F32)
    if has_bias:
        acc = acc + b_ref[...]
    o_ref[...] = acc.astype(o_ref.dtype)


def _mm(a, w, layer, *, col0=0, n=None, bias=None, out_dtype=BF16, tm=1024, tn=1024, name="mm"):
    M, K = a.shape
    n = w.shape[2] - col0 if n is None else n
    tm, tn = min(tm, M), min(tn, n)
    assert M % tm == 0 and n % tn == 0 and col0 % tn == 0
    cb = col0 // tn
    in_specs = [pl.BlockSpec((tm, K), lambda j, i: (i, 0)),
                pl.BlockSpec((None, K, tn), lambda j, i: (layer, 0, cb + j))]
    args = [a, w]
    if bias is not None:
        in_specs.append(pl.BlockSpec((None, 1, tn), lambda j, i: (layer, 0, cb + j)))
        args.append(bias.reshape(bias.shape[0], 1, -1))
    return pl.pallas_call(
        functools.partial(_mm_kernel, bias is not None),
        grid=(n // tn, M // tm),
        in_specs=in_specs,
        out_specs=pl.BlockSpec((tm, tn), lambda j, i: (i, j)),
        out_shape=jax.ShapeDtypeStruct((M, n), out_dtype),
        scratch_shapes=[pltpu.VMEM((K, tn), BF16)],
        compiler_params=_params("arbitrary", "arbitrary"),
        name=name,
    )(*args)


def _mm_res_norm_kernel(emit_h, rc, a_ref, w_ref, h_ref, g_ref, *outs):
    hn_ref = outs[-1]
    g = g_ref[...]
    for c in range(a_ref.shape[0] // rc):
        rows = slice(c * rc, (c + 1) * rc)
        acc = h_ref[rows, :] + jnp.dot(a_ref[rows, :], w_ref[...], preferred_element_type=F32)
        if emit_h:
            outs[0][rows, :] = acc
        hn_ref[rows, :] = _rms(acc, g).astype(hn_ref.dtype)


def _mm_res_norm(a, w, layer, h, g, *, emit_h=True, norm_dtype=BF16, tm=512, name="mm_res_norm"):
    M, K = a.shape
    D = w.shape[2]
    tm = min(tm, M)
    rc = min(ROW_CHUNK, tm // 2)
    assert M % tm == 0 and tm % rc == 0
    row = pl.BlockSpec((tm, D), lambda i: (i, 0))
    out_shape = [jax.ShapeDtypeStruct((M, D), norm_dtype)]
    out_specs = [row]
    if emit_h:
        out_shape.insert(0, jax.ShapeDtypeStruct((M, D), F32))
        out_specs.insert(0, row)
    res = pl.pallas_call(
        functools.partial(_mm_res_norm_kernel, emit_h, rc),
        grid=(M // tm,),
        in_specs=[pl.BlockSpec((tm, K), lambda i: (i, 0)),
                  pl.BlockSpec((None, K, D), lambda i: (layer, 0, 0), pipeline_mode=pl.Buffered(1)),
                  row,
                  pl.BlockSpec((1, D), lambda i: (0, 0))],
        out_specs=out_specs,
        out_shape=out_shape,
        compiler_params=_params("arbitrary"),
        name=name,
    )(a, w, h, g.reshape(1, D))
    return res if emit_h else (None, res[0])


def _conv3_rows(p, prev8, w_ref):
    w0, w1, w2 = w_ref[0:1, :], w_ref[1:2, :], w_ref[2:3, :]
    full = w2 * p + w1 * pltpu.roll(p, 1, 0) + w0 * pltpu.roll(p, 2, 0)
    e = jnp.concatenate([prev8, p[0:SUBLANES]], axis=0)
    f = w2 * e + w1 * pltpu.roll(e, 1, 0) + w0 * pltpu.roll(e, 2, 0)
    return full, f[SUBLANES:2 * SUBLANES]


def _load_halo(halo_ref, tiles_per_seq):
    @pl.when(pl.program_id(1) % tiles_per_seq == 0)
    def _():
        halo_ref[...] = jnp.zeros_like(halo_ref)
    return halo_ref[...]


def _mixer_a_kernel(tiles_per_seq, rc, a_ref, wb_ref, wc_ref, wx_ref, cw_ref, o_ref,
                    wb_s, wc_s, wx_s, halo_s):
    @pl.when(pl.program_id(1) == 0)
    def _():
        wb_s[...] = wb_ref[...].astype(BF16)
        wc_s[...] = wc_ref[...].astype(BF16)
        wx_s[...] = wx_ref[...].astype(BF16)

    prev8 = _load_halo(halo_s, tiles_per_seq)
    for c in range(a_ref.shape[0] // rc):
        r0 = c * rc
        a = a_ref[r0:r0 + rc, :]
        gb = jnp.dot(a, wb_s[...], preferred_element_type=F32)
        p = (jnp.dot(a, wc_s[...], preferred_element_type=F32)
             * jnp.dot(a, wx_s[...], preferred_element_type=F32))
        full, first8 = _conv3_rows(p, prev8, cw_ref)
        prev8 = p[rc - SUBLANES:rc]
        o_ref[r0:r0 + rc, :] = (gb * full).astype(o_ref.dtype)
        o_ref[r0:r0 + SUBLANES, :] = (gb[0:SUBLANES] * first8).astype(o_ref.dtype)
    halo_s[...] = prev8


def _mixer_a(hn, w_in, layer, conv_w, seq_len, *, tm=1024, tn=512):
    M, K = hn.shape
    ca = conv_w.shape[2]
    tm, tn = min(tm, seq_len), min(tn, ca)
    rc = min(ROW_CHUNK, tm)
    assert seq_len % tm == 0 and ca % tn == 0 and tm % rc == 0
    nj = ca // tn
    wspec = lambda off: pl.BlockSpec((None, K, tn), lambda j, i: (layer, 0, off * nj + j))
    return pl.pallas_call(
        functools.partial(_mixer_a_kernel, seq_len // tm, rc),
        grid=(nj, M // tm),
        in_specs=[pl.BlockSpec((tm, K), lambda j, i: (i, 0)),
                  wspec(0), wspec(1), wspec(2),
                  pl.BlockSpec((None, 3, tn), lambda j, i: (layer, 0, j))],
        out_specs=pl.BlockSpec((tm, tn), lambda j, i: (i, j)),
        out_shape=jax.ShapeDtypeStruct((M, ca), BF16),
        scratch_shapes=[pltpu.VMEM((K, tn), BF16)] * 3 + [pltpu.VMEM((SUBLANES, tn), F32)],
        compiler_params=_params("arbitrary", "arbitrary"),
        name="mixer_a",
    )(hn, w_in, w_in, w_in, conv_w)


def _ffn_up_kernel(tiles_per_seq, rc, a_ref, wg_ref, wu_ref, cw_ref, cb_ref, o_ref, wg_s, wu_s, halo_s):
    @pl.when(pl.program_id(1) == 0)
    def _():
        wg_s[...] = wg_ref[...].astype(BF16)
        wu_s[...] = wu_ref[...].astype(BF16)

    b = cb_ref[...]
    prev8 = _load_halo(halo_s, tiles_per_seq)
    for c in range(a_ref.shape[0] // rc):
        r0 = c * rc
        a = a_ref[r0:r0 + rc, :]
        gate = jnp.dot(a, wg_s[...], preferred_element_type=F32)
        up = jnp.dot(a, wu_s[...], preferred_element_type=F32)
        full, first8 = _conv3_rows(gate, prev8, cw_ref)
        prev8 = gate[rc - SUBLANES:rc]
        o_ref[r0:r0 + rc, :] = (jax.nn.silu(full + b) * up).astype(o_ref.dtype)
        o_ref[r0:r0 + SUBLANES, :] = (jax.nn.silu(first8 + b) * up[0:SUBLANES]).astype(o_ref.dtype)
    halo_s[...] = prev8


def _ffn_up(hn, w_gate, w_up, layer, conv_w, conv_b, seq_len, *, tm=1024, tn=512):
    M, K = hn.shape
    dff = w_gate.shape[2]
    tm, tn = min(tm, seq_len), min(tn, dff)
    rc = min(ROW_CHUNK, tm)
    assert seq_len % tm == 0 and dff % tn == 0 and tm % rc == 0
    wspec = pl.BlockSpec((None, K, tn), lambda j, i: (layer, 0, j))
    return pl.pallas_call(
        functools.partial(_ffn_up_kernel, seq_len // tm, rc),
        grid=(dff // tn, M // tm),
        in_specs=[pl.BlockSpec((tm, K), lambda j, i: (i, 0)),
                  wspec, wspec,
                  pl.BlockSpec((None, 3, tn), lambda j, i: (layer, 0, j)),
                  pl.BlockSpec((None, 1, tn), lambda j, i: (layer, 0, j))],
        out_specs=pl.BlockSpec((tm, tn), lambda j, i: (i, j)),
        out_shape=jax.ShapeDtypeStruct((M, dff), BF16),
        scratch_shapes=[pltpu.VMEM((K, tn), BF16)] * 2 + [pltpu.VMEM((SUBLANES, tn), F32)],
        compiler_params=_params("arbitrary", "arbitrary"),
        name="ffn_up",
    )(hn, w_gate, w_up, conv_w, conv_b.reshape(conv_b.shape[0], 1, dff))


def _s5_tables(a_re, a_im, log_dt, b_re, b_im, c_re, c_im, glu_w):
    hp = lax.Precision.HIGHEST
    T, gt = S5_CHUNK, GROUPS_PER_TILE
    G, P = a_re.shape
    H = b_re.shape[-1]
    J = G // gt
    dt = jnp.exp(log_dt)[:, None]
    lr, li = a_re * dt, a_im * dt

    def a_pow(taus):
        tau = jnp.asarray(np.asarray(taus, np.float32))[:, None, None]
        mag = jnp.exp(tau * lr)
        return mag * jnp.cos(tau * li), mag * jnp.sin(tau * li)

    pw_re, pw_im = a_pow(np.arange(T + 1))
    nr, ni = pw_re[1] - 1.0, pw_im[1]
    den = a_re * a_re + a_im * a_im
    qr, qi = (nr * a_re + ni * a_im) / den, (ni * a_re - nr * a_im) / den
    bb_re = qr[..., None] * b_re - qi[..., None] * b_im
    bb_im = qr[..., None] * b_im + qi[..., None] * b_re
    cp_re = c_re[None] * pw_re[:, :, None, :] - c_im[None] * pw_im[:, :, None, :]
    cp_im = c_re[None] * pw_im[:, :, None, :] + c_im[None] * pw_re[:, :, None, :]
    kern = (jnp.einsum('tgop,gph->tgoh', cp_re[:T], bb_re, precision=hp)
            - jnp.einsum('tgop,gph->tgoh', cp_im[:T], bb_im, precision=hp))
    rev_re, rev_im = a_pow(T - 1 - np.arange(T))
    wv_re = rev_re[..., None] * bb_re[None] - rev_im[..., None] * bb_im[None]
    wv_im = rev_re[..., None] * bb_im[None] + rev_im[..., None] * bb_re[None]

    def expand(compact, selector, row_group, col_group):
        full = jnp.dot(compact.astype(BF16), jnp.asarray(selector, BF16), preferred_element_type=F32)
        rows, cols = full.shape[-2:]
        rg = row_group(lax.broadcasted_iota(jnp.int32, (rows, cols), 0))
        cg = col_group(lax.broadcasted_iota(jnp.int32, (rows, cols), 1))
        return jnp.where(rg == cg, full, 0.0).astype(BF16)

    eye_h, eye_p = np.eye(H, dtype=np.float32), np.eye(P, dtype=np.float32)
    kc = kern.reshape(T, J, gt, H, H).transpose(1, 0, 2, 4, 3).reshape(J, T, gt * H, H)
    dtile = expand(kc, np.tile(eye_h, (1, gt)), lambda r: r // H, lambda c: c // H)
    def in_rows(x):
        return x.reshape(T, J, gt, P, H).transpose(1, 0, 2, 4, 3).reshape(J, T * LANES, P)
    wvc = jnp.concatenate([in_rows(wv_re), in_rows(wv_im)], axis=-1)
    wv = expand(wvc, np.kron(np.eye(2, dtype=np.float32), np.tile(eye_p, (1, gt))),
                lambda r: (r // H) % gt, lambda c: (c % (gt * P)) // P)
    def out_rows(x):
        return x.reshape(T, J, gt, H, P).transpose(1, 2, 4, 0, 3).reshape(J, gt * P, T * H)
    woc = jnp.concatenate([out_rows(cp_re[1:]), out_rows(-cp_im[1:])], axis=1)
    wo = expand(woc, np.kron(np.eye(T, dtype=np.float32), np.tile(eye_h, (1, gt))),
                lambda r: (r % (gt * P)) // P, lambda c: (c % LANES) // H)
    a_chunk = jnp.concatenate([pw_re[T].reshape(J, 1, gt * P), pw_im[T].reshape(J, 1, gt * P)], axis=-1)
    gc = glu_w.reshape(J, gt * H, H)
    glu = expand(gc, np.tile(eye_h, (1, gt)), lambda r: r // H, lambda c: c // H)
    return dtile, wv, wo, a_chunk, glu


def _gelu_tanh(x):
    c = math.sqrt(2.0 / math.pi)
    return 0.5 * x * (1.0 + jnp.tanh(c * (x + 0.044715 * (x * x * x))))


def _s5_kernel(u_ref, dt_ref, wv_ref, wo_ref, ac_ref, d_ref, glu_ref, o_ref,
               mi_s, ubf_s, v_s, sp_s, y_s, ynat_s):
    T = S5_CHUNK
    nc = u_ref.shape[0] // T
    ns = ac_ref.shape[-1] // 2
    pair = MXU_DIM // LANES

    @pl.when(pl.program_id(1) == 0)
    def _():
        for s in range(T):
            for t in range(T):
                if t >= s:
                    tile = dt_ref[0, t - s]
                elif t // pair == s // pair:
                    tile = jnp.zeros((LANES, LANES), BF16)
                else:
                    continue
                mi_s[s * LANES:(s + 1) * LANES, t * LANES:(t + 1) * LANES] = tile

    for t in range(T):
        ubf_s[:, t * LANES:(t + 1) * LANES] = u_ref[pl.ds(t, nc, stride=T), :].astype(BF16)
    v_s[...] = jnp.dot(ubf_s[...], wv_ref[0], preferred_element_type=F32)
    ar, ai = ac_ref[0, :, 0:ns], ac_ref[0, :, ns:2 * ns]

    def step(c, carry):
        sr, si = carry
        sp_s[pl.ds(c, 1), 0:ns] = sr
        sp_s[pl.ds(c, 1), ns:2 * ns] = si
        vr = v_s[pl.ds(c, 1), 0:ns]
        vi = v_s[pl.ds(c, 1), ns:2 * ns]
        return ar * sr - ai * si + vr, ar * si + ai * sr + vi

    zero = jnp.zeros((1, ns), F32)
    lax.fori_loop(0, nc, step, (zero, zero))
    sp = sp_s[...].astype(BF16)
    for cb in range(T // pair):
        kk = (cb + 1) * MXU_DIM
        cols = slice(cb * MXU_DIM, (cb + 1) * MXU_DIM)
        y_s[:, cols] = (jnp.dot(ubf_s[:, 0:kk], mi_s[0:kk, cols], preferred_element_type=F32)
                        + jnp.dot(sp, wo_ref[0, :, cols], preferred_element_type=F32))
    for t in range(T):
        ynat_s[pl.ds(t, nc, stride=T), :] = y_s[:, t * LANES:(t + 1) * LANES]
    y = ynat_s[...] + d_ref[...] * u_ref[...]
    yg = _gelu_tanh(y)
    gate = jnp.dot(yg.astype(BF16), glu_ref[0], preferred_element_type=F32)
    o_ref[...] = (yg * jax.nn.sigmoid(gate)).astype(o_ref.dtype)


def _s5(u, tables, d, layer, batch, seq_len):
    dtile, wv, wo, a_chunk, glu = tables
    J = wv.shape[0]
    M, width = u.shape
    T = S5_CHUNK
    nc = seq_len // T
    n2 = wv.shape[-1]
    return pl.pallas_call(
        _s5_kernel,
        grid=(J, batch),
        in_specs=[pl.BlockSpec((seq_len, LANES), lambda j, b: (b, j)),
                  pl.BlockSpec((1, T, LANES, LANES), lambda j, b: (j, 0, 0, 0)),
                  pl.BlockSpec((1, T * LANES, n2), lambda j, b: (j, 0, 0)),
                  pl.BlockSpec((1, n2, T * LANES), lambda j, b: (j, 0, 0)),
                  pl.BlockSpec((1, 1, n2), lambda j, b: (j, 0, 0)),
                  pl.BlockSpec((None, 1, LANES), lambda j, b: (layer, 0, j)),
                  pl.BlockSpec((1, LANES, LANES), lambda j, b: (j, 0, 0))],
        out_specs=pl.BlockSpec((seq_len, LANES), lambda j, b: (b, j)),
        out_shape=jax.ShapeDtypeStruct((M, width), BF16),
        scratch_shapes=[pltpu.VMEM((T * LANES, T * LANES), BF16),
                        pltpu.VMEM((nc, T * LANES), BF16),
                        pltpu.VMEM((nc, n2), F32),
                        pltpu.VMEM((nc, n2), F32),
                        pltpu.VMEM((nc, T * LANES), F32),
                        pltpu.VMEM((seq_len, LANES), F32)],
        compiler_params=_params("arbitrary", "arbitrary"),
        name="s5",
    )(u, dtile, wv, wo, a_chunk, d.reshape(d.shape[0], 1, width), glu)


def _t5_bucket_np(rel):
    max_exact = N_BUCKETS // 2
    n = np.maximum(rel, 0)
    nf = np.maximum(n, max_exact).astype(np.float32)
    large = max_exact + (np.log(nf / np.float32(max_exact)) / np.float32(math.log(MAX_DISTANCE / max_exact))
                         * np.float32(N_BUCKETS - max_exact)).astype(np.int32)
    large = np.minimum(large, N_BUCKETS - 1)
    return np.where(n < max_exact, n, large).astype(np.int32)


def _swa_table_kernel(bias_ref, sink_ref, bucket_ref, valid_ref, add_ref, mul_ref):
    h = pl.program_id(1)
    bucket = bucket_ref[...]
    valid = valid_ref[0] != 0
    acc = jnp.full(bucket.shape, NEG_INF, F32)
    for b in range(N_BUCKETS):
        acc = jnp.where(bucket == b, bias_ref[b, h], acc)
    acc = jnp.where(valid, acc, NEG_INF)
    col = lax.broadcasted_iota(jnp.int32, bucket.shape, 1)
    add_ref[0, 0] = jnp.where(col == 0, sink_ref[h], acc)
    mul_ref[0, 0] = jnp.where(valid, HEAD_DIM ** -0.5, 0.0).astype(F32)


def _swa_tables(rel_bias, sinks):
    blk = ATT_BLOCK
    qi = np.arange(blk)[:, None]
    kj = np.arange(2 * blk)[None, :]
    rel = qi + blk - kj
    in_window = (rel >= 0) & (rel < blk)
    valid = np.stack([in_window & (kj >= blk), in_window]).astype(np.int32)
    bucket = _t5_bucket_np(rel)
    shape = jax.ShapeDtypeStruct((2, N_Q_HEADS, blk, 2 * blk), F32)
    tab = pl.BlockSpec((1, 1, blk, 2 * blk), lambda v, h: (v, h, 0, 0))
    return pl.pallas_call(
        _swa_table_kernel,
        grid=(2, N_Q_HEADS),
        in_specs=[pl.BlockSpec(memory_space=pltpu.SMEM),
                  pl.BlockSpec(memory_space=pltpu.SMEM),
                  pl.BlockSpec((blk, 2 * blk), lambda v, h: (0, 0)),
                  pl.BlockSpec((1, blk, 2 * blk), lambda v, h: (v, 0, 0))],
        out_specs=[tab, tab],
        out_shape=[shape, shape],
        compiler_params=_params("arbitrary", "arbitrary"),
        name="swa_tables",
    )(rel_bias, sinks, jnp.asarray(bucket), jnp.asarray(valid))


def _swa_kernel(q_ref, kp_ref, kc_ref, vp_ref, vc_ref, add_ref, mul_ref, o_ref):
    blk, hd, g = ATT_BLOCK, HEAD_DIM, Q_PER_KV
    first_row = lax.broadcasted_iota(jnp.int32, (2 * blk, hd), 0) == 0
    outs = []
    for k in range(N_KV_HEADS):
        sl = slice(k * hd, (k + 1) * hd)
        keys = jnp.concatenate([kp_ref[:, sl], kc_ref[:, sl]], axis=0)
        vals = jnp.concatenate([vp_ref[:, sl], vc_ref[:, sl]], axis=0)
        vals = jnp.where(first_row, jnp.zeros_like(vals), vals)
        q = jnp.concatenate([q_ref[:, (k * g + i) * hd:(k * g + i + 1) * hd] for i in range(g)],
                            axis=0)
        s = lax.dot_general(q, keys, (((1,), (1,)), ((), ())), preferred_element_type=F32)
        s = (s * mul_ref[0, k * g:(k + 1) * g].reshape(g * blk, 2 * blk)
             + add_ref[0, k * g:(k + 1) * g].reshape(g * blk, 2 * blk))
        m = jnp.max(s, axis=-1, keepdims=True)
        p = jnp.exp(s - m)
        den = jnp.sum(p, axis=-1, keepdims=True)
        o = jnp.dot(p.astype(BF16), vals, preferred_element_type=F32) / den
        outs.extend(o[i * blk:(i + 1) * blk] for i in range(g))
    o_ref[...] = jnp.concatenate(outs, axis=-1).astype(o_ref.dtype)


def _swa(z, tables, batch, seq_len):
    add, mul = tables
    blk = ATT_BLOCK
    nblk = seq_len // blk
    M = z.shape[0]
    dq = N_Q_HEADS * HEAD_DIM
    dkv = N_KV_HEADS * HEAD_DIM
    kcol, vcol = dq // dkv, dq // dkv + 1
    cur = lambda c: pl.BlockSpec((blk, dkv), lambda b, n: (b * nblk + n, c))
    prev = lambda c: pl.BlockSpec((blk, dkv), lambda b, n: (b * nblk + jnp.maximum(n - 1, 0), c))
    tab = pl.BlockSpec((1, N_Q_HEADS, blk, 2 * blk), lambda b, n: (jnp.minimum(n, 1), 0, 0, 0))
    return pl.pallas_call(
        _swa_kernel,
        grid=(batch, nblk),
        in_specs=[pl.BlockSpec((blk, dq), lambda b, n: (b * nblk + n, 0)),
                  prev(kcol), cur(kcol), prev(vcol), cur(vcol), tab, tab],
        out_specs=pl.BlockSpec((blk, dq), lambda b, n: (b * nblk + n, 0)),
        out_shape=jax.ShapeDtypeStruct((M, dq), BF16),
        compiler_params=_params("arbitrary", "arbitrary"),
        name="swa",
    )(z, z, z, z, z, add, mul)


def _xattn_kernel(q_ref, kv_ref, o_ref):
    d = q_ref.shape[1]
    hd = d // X_HEADS
    outs = []
    for h in range(X_HEADS):
        q = q_ref[:, h * hd:(h + 1) * hd]
        k = kv_ref[:, h * hd:(h + 1) * hd]
        v = kv_ref[:, d + h * hd:d + (h + 1) * hd]
        s = lax.dot_general(q, k, (((1,), (1,)), ((), ())), preferred_element_type=F32) * (hd ** -0.5)
        m = jnp.max(s, axis=-1, keepdims=True)
        p = jnp.exp(s - m)
        den = jnp.sum(p, axis=-1, keepdims=True)
        outs.append(jnp.dot(p.astype(BF16), v, preferred_element_type=F32) / den)
    o_ref[...] = jnp.concatenate(outs, axis=-1).astype(o_ref.dtype)


def _xattn(q, kv, batch, seq_len, n_mem, *, tq=512):
    M, d = q.shape
    tq = min(tq, seq_len)
    nq = seq_len // tq
    return pl.pallas_call(
        _xattn_kernel,
        grid=(batch, nq),
        in_specs=[pl.BlockSpec((tq, d), lambda b, i: (b * nq + i, 0)),
                  pl.BlockSpec((n_mem, 2 * d), lambda b, i: (b, 0))],
        out_specs=pl.BlockSpec((tq, d), lambda b, i: (b * nq + i, 0)),
        out_shape=jax.ShapeDtypeStruct((M, d), BF16),
        compiler_params=_params("arbitrary", "arbitrary"),
        name="xattn",
    )(q, kv)


def kernel(x, mem, norm_mix, norm_xattn, norm_ffn, norm_final, norm_mem, rel_bias, ev_w_in, ev_conv_w, s5_a_re, s5_a_im, s5_log_dt, s5_b_re, s5_b_im, s5_c_re, s5_c_im, s5_d, s5_glu_w, ev_w_out, od_w_qkv, od_b_qkv, od_sinks, od_w_out, xa_w_q, xa_w_kv, xa_w_o, ff_w_gate, ff_w_up, ff_conv_w, ff_conv_b, ff_w_down):
    batch, seq_len, d = x.shape
    n_mem = mem.shape[1]
    depth = norm_mix.shape[0]
    M = batch * seq_len
    a_width = ev_conv_w.shape[-1]
    ev_w_out, od_w_out, xa_w_o, ff_w_down = (w.astype(BF16) for w in (ev_w_out, od_w_out, xa_w_o, ff_w_down))

    mem_n = _rmsnorm(mem.reshape(batch * n_mem, d), norm_mem)
    h = x.reshape(M, d)
    hn = _rmsnorm(h, norm_mix[0])
    for l in range(depth):
        i = l // 2
        if l % 2 == 0:
            ya = _mixer_a(hn, ev_w_in, i, ev_conv_w, seq_len)
            u = _mm(hn, ev_w_in, i, col0=3 * a_width, out_dtype=F32, name="s5_in")
            tables = _s5_tables(s5_a_re[i], s5_a_im[i], s5_log_dt[i], s5_b_re[i], s5_b_im[i],
                                s5_c_re[i], s5_c_im[i], s5_glu_w[i])
            ys = _s5(u, tables, s5_d, i, batch, seq_len)
            mixed = jnp.concatenate([ya, ys], axis=-1)
            h, hn = _mm_res_norm(mixed, ev_w_out, i, h, norm_xattn[l], name="ev_out")
        else:
            z = _mm(hn, od_w_qkv, i, bias=od_b_qkv, tn=1280, name="qkv")
            o = _swa(z, _swa_tables(rel_bias, od_sinks[i]), batch, seq_len)
            h, hn = _mm_res_norm(o, od_w_out, i, h, norm_xattn[l], name="od_out")
        q = _mm(hn, xa_w_q, l, name="xa_q")
        kv = _mm(mem_n, xa_w_kv, l, name="xa_kv")
        o = _xattn(q, kv, batch, seq_len, n_mem)
        h, hn = _mm_res_norm(o, xa_w_o, l, h, norm_ffn[l], name="xa_out")
        act = _ffn_up(hn, ff_w_gate, ff_w_up, l, ff_conv_w, ff_conv_b, seq_len)
        if l + 1 < depth:
            h, hn = _mm_res_norm(act, ff_w_down, l, h, norm_mix[l + 1], tm=256, name="ff_down")
        else:
            _, out = _mm_res_norm(act, ff_w_down, l, h, norm_final, emit_h=False, norm_dtype=F32,
                                  tm=256, name="ff_down_final")
    return out.reshape(batch, seq_len, d)
```

```python
import functools
import math

import numpy as np
import jax
import jax.numpy as jnp
from jax import lax
from jax.experimental import pallas as pl
from jax.experimental.pallas import tpu as pltpu

F32 = jnp.float32
BF16 = jnp.bfloat16

RMS_EPS = 1e-5
NEG_INF = -1e30

HEAD_DIM = 64
Q_PER_KV = 8
N_KV_HEADS = 4
N_Q_HEADS = Q_PER_KV * N_KV_HEADS
ATT_BLOCK = 128
N_BUCKETS = 32
MAX_DISTANCE = 128
X_HEADS = 4
S5_GROUP = 16
S5_STATE = 64

LANES = 128
SUBLANES = 8
MXU_DIM = 256
VMEM_LIMIT_BYTES = 60000 * 1024

S5_CHUNK = 16
GROUPS_PER_TILE = LANES // S5_GROUP

ROW_CHUNK = 256


def _params(*sem):
    return pltpu.CompilerParams(dimension_semantics=sem, vmem_limit_bytes=VMEM_LIMIT_BYTES)


def _rms(x, g):
    ms = jnp.mean(x * x, axis=-1, keepdims=True)
    return x * lax.rsqrt(ms + RMS_EPS) * g


def _rmsnorm_kernel(x_ref, g_ref, o_ref):
    o_ref[...] = _rms(x_ref[...], g_ref[...]).astype(o_ref.dtype)


def _rmsnorm(x, g, *, tm=512):
    M, D = x.shape
    tm = min(tm, M)
    return pl.pallas_call(
        _rmsnorm_kernel,
        grid=(M // tm,),
        in_specs=[pl.BlockSpec((tm, D), lambda i: (i, 0)), pl.BlockSpec((1, D), lambda i: (0, 0))],
        out_specs=pl.BlockSpec((tm, D), lambda i: (i, 0)),
        out_shape=jax.ShapeDtypeStruct((M, D), BF16),
        compiler_params=_params("arbitrary"),
        name="rmsnorm",
    )(x, g.reshape(1, D))


def _mm_kernel(has_bias, rc, a_ref, w_ref, *rest):
    if has_bias:
        b_ref, o_ref, wbf_ref = rest
    else:
        o_ref, wbf_ref = rest

    @pl.when(pl.program_id(1) == 0)
    def _():
        wbf_ref[...] = w_ref[...].astype(BF16)

    for c in range(a_ref.shape[0] // rc):
        rows = slice(c * rc, (c + 1) * rc)
        acc = jnp.dot(a_ref[rows, :], wbf_ref[...], preferred_element_type=F32)
        if has_bias:
            acc = acc + b_ref[...]
        o_ref[rows, :] = acc.astype(o_ref.dtype)


def _mm(a, w, layer, *, col0=0, n=None, bias=None, out_dtype=BF16, tm=2048, tn=1024, name="mm"):
    M, K = a.shape
    n = w.shape[2] - col0 if n is None else n
    tm, tn = min(tm, M), min(tn, n)
    rc = min(2 * ROW_CHUNK, tm)
    assert M % tm == 0 and n % tn == 0 and col0 % tn == 0 and tm % rc == 0
    cb = col0 // tn
    in_specs = [pl.BlockSpec((tm, K), lambda j, i: (i, 0)),
                pl.BlockSpec((None, K, tn), lambda j, i: (layer, 0, cb + j))]
    args = [a, w]
    if bias is not None:
        in_specs.append(pl.BlockSpec((None, 1, tn), lambda j, i: (layer, 0, cb + j)))
        args.append(bias.reshape(bias.shape[0], 1, -1))
    return pl.pallas_call(
        functools.partial(_mm_kernel, bias is not None, rc),
        grid=(n // tn, M // tm),
        in_specs=in_specs,
        out_specs=pl.BlockSpec((tm, tn), lambda j, i: (i, j)),
        out_shape=jax.ShapeDtypeStruct((M, n), out_dtype),
        scratch_shapes=[pltpu.VMEM((K, tn), BF16)],
        compiler_params=_params("arbitrary", "arbitrary"),
        name=name,
    )(*args)


def _mm_res_norm_kernel(emit_h, rc, n_a, *refs):
    a_refs = refs[:n_a]
    w_ref, h_ref, g_ref = refs[n_a:n_a + 3]
    outs = refs[n_a + 3:]
    hn_ref = outs[-1]
    g = g_ref[...]
    for c in range(h_ref.shape[0] // rc):
        rows = slice(c * rc, (c + 1) * rc)
        acc = h_ref[rows, :]
        k0 = 0
        for a_ref in a_refs:
            k1 = k0 + a_ref.shape[1]
            acc = acc + jnp.dot(a_ref[rows, :], w_ref[k0:k1, :], preferred_element_type=F32)
            k0 = k1
        if emit_h:
            outs[0][rows, :] = acc
        hn_ref[rows, :] = _rms(acc, g).astype(hn_ref.dtype)


def _mm_res_norm(a_parts, w, layer, h, g, *, emit_h=True, norm_dtype=BF16, tm=512, name="mm_res_norm"):
    M, D = h.shape
    assert sum(a.shape[1] for a in a_parts) == w.shape[1]
    tm = min(tm, M)
    rc = min(ROW_CHUNK, tm // 2)
    assert M % tm == 0 and tm % rc == 0
    row = pl.BlockSpec((tm, D), lambda i: (i, 0))
    out_shape = [jax.ShapeDtypeStruct((M, D), norm_dtype)]
    out_specs = [row]
    if emit_h:
        out_shape.insert(0, jax.ShapeDtypeStruct((M, D), F32))
        out_specs.insert(0, row)
    res = pl.pallas_call(
        functools.partial(_mm_res_norm_kernel, emit_h, rc, len(a_parts)),
        grid=(M // tm,),
        in_specs=[pl.BlockSpec((tm, a.shape[1]), lambda i: (i, 0)) for a in a_parts] + [
            pl.BlockSpec((None,) + w.shape[1:], lambda i: (layer, 0, 0), pipeline_mode=pl.Buffered(1)),
            row,
            pl.BlockSpec((1, D), lambda i: (0, 0))],
        out_specs=out_specs,
        out_shape=out_shape,
        compiler_params=_params("arbitrary"),
        name=name,
    )(*a_parts, w, h, g.reshape(1, D))
    return res if emit_h else (None, res[0])


def _conv3_rows(p, prev8, w_ref):
    w0, w1, w2 = w_ref[0:1, :], w_ref[1:2, :], w_ref[2:3, :]
    full = w2 * p + w1 * pltpu.roll(p, 1, 0) + w0 * pltpu.roll(p, 2, 0)
    e = jnp.concatenate([prev8, p[0:SUBLANES]], axis=0)
    f = w2 * e + w1 * pltpu.roll(e, 1, 0) + w0 * pltpu.roll(e, 2, 0)
    return full, f[SUBLANES:2 * SUBLANES]


def _load_halo(halo_ref, tiles_per_seq):
    @pl.when(pl.program_id(1) % tiles_per_seq == 0)
    def _():
        halo_ref[...] = jnp.zeros_like(halo_ref)
    return halo_ref[...]


def _mixer_a_kernel(tiles_per_seq, rc, a_ref, wb_ref, wc_ref, wx_ref, cw_ref, o_ref,
                    wb_s, wc_s, wx_s, halo_s):
    @pl.when(pl.program_id(1) == 0)
    def _():
        wb_s[...] = wb_ref[...].astype(BF16)
        wc_s[...] = wc_ref[...].astype(BF16)
        wx_s[...] = wx_ref[...].astype(BF16)

    prev8 = _load_halo(halo_s, tiles_per_seq)
    for c in range(a_ref.shape[0] // rc):
        r0 = c * rc
        a = a_ref[r0:r0 + rc, :]
        gb = jnp.dot(a, wb_s[...], preferred_element_type=F32)
        p = (jnp.dot(a, wc_s[...], preferred_element_type=F32)
             * jnp.dot(a, wx_s[...], preferred_element_type=F32))
        full, first8 = _conv3_rows(p, prev8, cw_ref)
        prev8 = p[rc - SUBLANES:rc]
        o_ref[r0:r0 + rc, :] = (gb * full).astype(o_ref.dtype)
        o_ref[r0:r0 + SUBLANES, :] = (gb[0:SUBLANES] * first8).astype(o_ref.dtype)
    halo_s[...] = prev8


def _mixer_a(hn, w_in, layer, conv_w, seq_len, *, tm=2048, tn=512):
    M, K = hn.shape
    ca = conv_w.shape[2]
    tm, tn = min(tm, seq_len), min(tn, ca)
    rc = min(ROW_CHUNK, tm)
    assert seq_len % tm == 0 and ca % tn == 0 and tm % rc == 0
    nj = ca // tn
    wspec = lambda off: pl.BlockSpec((None, K, tn), lambda j, i: (layer, 0, off * nj + j))
    return pl.pallas_call(
        functools.partial(_mixer_a_kernel, seq_len // tm, rc),
        grid=(nj, M // tm),
        in_specs=[pl.BlockSpec((tm, K), lambda j, i: (i, 0)),
                  wspec(0), wspec(1), wspec(2),
                  pl.BlockSpec((None, 3, tn), lambda j, i: (layer, 0, j))],
        out_specs=pl.BlockSpec((tm, tn), lambda j, i: (i, j)),
        out_shape=jax.ShapeDtypeStruct((M, ca), BF16),
        scratch_shapes=[pltpu.VMEM((K, tn), BF16)] * 3 + [pltpu.VMEM((SUBLANES, tn), F32)],
        compiler_params=_params("arbitrary", "arbitrary"),
        name="mixer_a",
    )(hn, w_in, w_in, w_in, conv_w)


def _ffn_up_kernel(tiles_per_seq, rc, a_ref, wg_ref, wu_ref, cw_ref, cb_ref, o_ref, wg_s, wu_s, halo_s):
    @pl.when(pl.program_id(1) == 0)
    def _():
        wg_s[...] = wg_ref[...].astype(BF16)
        wu_s[...] = wu_ref[...].astype(BF16)

    b = cb_ref[...]
    prev8 = _load_halo(halo_s, tiles_per_seq)
    for c in range(a_ref.shape[0] // rc):
        r0 = c * rc
        a = a_ref[r0:r0 + rc, :]
        gate = jnp.dot(a, wg_s[...], preferred_element_type=F32)
        up = jnp.dot(a, wu_s[...], preferred_element_type=F32)
        full, first8 = _conv3_rows(gate, prev8, cw_ref)
        prev8 = gate[rc - SUBLANES:rc]
        o_ref[r0:r0 + rc, :] = (jax.nn.silu(full + b) * up).astype(o_ref.dtype)
        o_ref[r0:r0 + SUBLANES, :] = (jax.nn.silu(first8 + b) * up[0:SUBLANES]).astype(o_ref.dtype)
    halo_s[...] = prev8


def _ffn_up(hn, w_gate, w_up, layer, conv_w, conv_b, seq_len, *, tm=2048, tn=512):
    M, K = hn.shape
    dff = w_gate.shape[2]
    tm, tn = min(tm, seq_len), min(tn, dff)
    rc = min(ROW_CHUNK, tm)
    assert seq_len % tm == 0 and dff % tn == 0 and tm % rc == 0
    wspec = pl.BlockSpec((None, K, tn), lambda j, i: (layer, 0, j))
    return pl.pallas_call(
        functools.partial(_ffn_up_kernel, seq_len // tm, rc),
        grid=(dff // tn, M // tm),
        in_specs=[pl.BlockSpec((tm, K), lambda j, i: (i, 0)),
                  wspec, wspec,
                  pl.BlockSpec((None, 3, tn), lambda j, i: (layer, 0, j)),
                  pl.BlockSpec((None, 1, tn), lambda j, i: (layer, 0, j))],
        out_specs=pl.BlockSpec((tm, tn), lambda j, i: (i, j)),
        out_shape=jax.ShapeDtypeStruct((M, dff), BF16),
        scratch_shapes=[pltpu.VMEM((K, tn), BF16)] * 2 + [pltpu.VMEM((SUBLANES, tn), F32)],
        compiler_params=_params("arbitrary", "arbitrary"),
        name="ffn_up",
    )(hn, w_gate, w_up, conv_w, conv_b.reshape(conv_b.shape[0], 1, dff))


def _s5_tables(a_re, a_im, log_dt, b_re, b_im, c_re, c_im, glu_w):
    hp = lax.Precision.HIGHEST
    T, gt = S5_CHUNK, GROUPS_PER_TILE
    G, P = a_re.shape
    H = b_re.shape[-1]
    J = G // gt
    dt = jnp.exp(log_dt)[:, None]
    lr, li = a_re * dt, a_im * dt

    def a_pow(taus):
        tau = jnp.asarray(np.asarray(taus, np.float32))[:, None, None]
        mag = jnp.exp(tau * lr)
        return mag * jnp.cos(tau * li), mag * jnp.sin(tau * li)

    pw_re, pw_im = a_pow(np.arange(T + 1))
    nr, ni = pw_re[1] - 1.0, pw_im[1]
    den = a_re * a_re + a_im * a_im
    qr, qi = (nr * a_re + ni * a_im) / den, (ni * a_re - nr * a_im) / den
    bb_re = qr[..., None] * b_re - qi[..., None] * b_im
    bb_im = qr[..., None] * b_im + qi[..., None] * b_re
    cp_re = c_re[None] * pw_re[:, :, None, :] - c_im[None] * pw_im[:, :, None, :]
    cp_im = c_re[None] * pw_im[:, :, None, :] + c_im[None] * pw_re[:, :, None, :]
    kern = (jnp.einsum('tgop,gph->tgoh', cp_re[:T], bb_re, precision=hp)
            - jnp.einsum('tgop,gph->tgoh', cp_im[:T], bb_im, precision=hp))
    rev_re, rev_im = a_pow(T - 1 - np.arange(T))
    wv_re = rev_re[..., None] * bb_re[None] - rev_im[..., None] * bb_im[None]
    wv_im = rev_re[..., None] * bb_im[None] + rev_im[..., None] * bb_re[None]

    def expand(compact, selector, row_group, col_group):
        full = jnp.dot(compact.astype(BF16), jnp.asarray(selector, BF16), preferred_element_type=F32)
        rows, cols = full.shape[-2:]
        rg = row_group(lax.broadcasted_iota(jnp.int32, (rows, cols), 0))
        cg = col_group(lax.broadcasted_iota(jnp.int32, (rows, cols), 1))
        return jnp.where(rg == cg, full, 0.0).astype(BF16)

    eye_h, eye_p = np.eye(H, dtype=np.float32), np.eye(P, dtype=np.float32)
    kc = kern.reshape(T, J, gt, H, H).transpose(1, 0, 2, 4, 3).reshape(J, T, gt * H, H)
    dtile = expand(kc, np.tile(eye_h, (1, gt)), lambda r: r // H, lambda c: c // H)
    def in_rows(x):
        return x.reshape(T, J, gt, P, H).transpose(1, 0, 2, 4, 3).reshape(J, T * LANES, P)
    wvc = jnp.concatenate([in_rows(wv_re), in_rows(wv_im)], axis=-1)
    wv = expand(wvc, np.kron(np.eye(2, dtype=np.float32), np.tile(eye_p, (1, gt))),
                lambda r: (r // H) % gt, lambda c: (c % (gt * P)) // P)
    def out_rows(x):
        return x.reshape(T, J, gt, H, P).transpose(1, 2, 4, 0, 3).reshape(J, gt * P, T * H)
    woc = jnp.concatenate([out_rows(cp_re[1:]), out_rows(-cp_im[1:])], axis=1)
    wo = expand(woc, np.kron(np.eye(T, dtype=np.float32), np.tile(eye_h, (1, gt))),
                lambda r: (r % (gt * P)) // P, lambda c: (c % LANES) // H)
    a_chunk = jnp.concatenate([pw_re[T].reshape(J, 1, gt * P), pw_im[T].reshape(J, 1, gt * P)], axis=-1)
    gc = glu_w.reshape(J, gt * H, H)
    glu = expand(gc, np.tile(eye_h, (1, gt)), lambda r: r // H, lambda c: c // H)
    return dtile, wv, wo, a_chunk, glu


def _gelu_tanh(x):
    c = math.sqrt(2.0 / math.pi)
    return 0.5 * x * (1.0 + jnp.tanh(c * (x + 0.044715 * (x * x * x))))


def _s5_kernel(u_ref, dt_ref, wv_ref, wo_ref, ac_ref, d_ref, glu_ref, o_ref,
               mi_s, ubf_s, v_s, sp_s, y_s, ynat_s):
    T = S5_CHUNK
    nc = u_ref.shape[0] // T
    ns = ac_ref.shape[-1] // 2
    pair = MXU_DIM // LANES

    @pl.when(pl.program_id(1) == 0)
    def _():
        for s in range(T):
            for t in range(T):
                if t >= s:
                    tile = dt_ref[0, t - s]
                elif t // pair == s // pair:
                    tile = jnp.zeros((LANES, LANES), BF16)
                else:
                    continue
                mi_s[s * LANES:(s + 1) * LANES, t * LANES:(t + 1) * LANES] = tile

    for t in range(T):
        ubf_s[:, t * LANES:(t + 1) * LANES] = u_ref[pl.ds(t, nc, stride=T), :].astype(BF16)
    v_s[...] = jnp.dot(ubf_s[...], wv_ref[0], preferred_element_type=F32)
    ar, ai = ac_ref[0, :, 0:ns], ac_ref[0, :, ns:2 * ns]

    def step(c, carry):
        sr, si = carry
        sp_s[pl.ds(c, 1), 0:ns] = sr
        sp_s[pl.ds(c, 1), ns:2 * ns] = si
        vr = v_s[pl.ds(c, 1), 0:ns]
        vi = v_s[pl.ds(c, 1), ns:2 * ns]
        return ar * sr - ai * si + vr, ar * si + ai * sr + vi

    zero = jnp.zeros((1, ns), F32)
    lax.fori_loop(0, nc, step, (zero, zero))
    sp = sp_s[...].astype(BF16)
    for cb in range(T // pair):
        kk = (cb + 1) * MXU_DIM
        cols = slice(cb * MXU_DIM, (cb + 1) * MXU_DIM)
        y_s[:, cols] = (jnp.dot(ubf_s[:, 0:kk], mi_s[0:kk, cols], preferred_element_type=F32)
                        + jnp.dot(sp, wo_ref[0, :, cols], preferred_element_type=F32))
    for t in range(T):
        ynat_s[pl.ds(t, nc, stride=T), :] = y_s[:, t * LANES:(t + 1) * LANES]
    y = ynat_s[...] + d_ref[...] * u_ref[...]
    yg = _gelu_tanh(y)
    gate = jnp.dot(yg.astype(BF16), glu_ref[0], preferred_element_type=F32)
    o_ref[...] = (yg * jax.nn.sigmoid(gate)).astype(o_ref.dtype)


def _s5(u, tables, d, layer, batch, seq_len):
    dtile, wv, wo, a_chunk, glu = tables
    J = wv.shape[0]
    M, width = u.shape
    T = S5_CHUNK
    nc = seq_len // T
    n2 = wv.shape[-1]
    return pl.pallas_call(
        _s5_kernel,
        grid=(J, batch),
        in_specs=[pl.BlockSpec((seq_len, LANES), lambda j, b: (b, j)),
                  pl.BlockSpec((1, T, LANES, LANES), lambda j, b: (j, 0, 0, 0)),
                  pl.BlockSpec((1, T * LANES, n2), lambda j, b: (j, 0, 0)),
                  pl.BlockSpec((1, n2, T * LANES), lambda j, b: (j, 0, 0)),
                  pl.BlockSpec((1, 1, n2), lambda j, b: (j, 0, 0)),
                  pl.BlockSpec((None, 1, LANES), lambda j, b: (layer, 0, j)),
                  pl.BlockSpec((1, LANES, LANES), lambda j, b: (j, 0, 0))],
        out_specs=pl.BlockSpec((seq_len, LANES), lambda j, b: (b, j)),
        out_shape=jax.ShapeDtypeStruct((M, width), BF16),
        scratch_shapes=[pltpu.VMEM((T * LANES, T * LANES), BF16),
                        pltpu.VMEM((nc, T * LANES), BF16),
                        pltpu.VMEM((nc, n2), F32),
                        pltpu.VMEM((nc, n2), F32),
                        pltpu.VMEM((nc, T * LANES), F32),
                        pltpu.VMEM((seq_len, LANES), F32)],
        compiler_params=_params("arbitrary", "arbitrary"),
        name="s5",
    )(u, dtile, wv, wo, a_chunk, d.reshape(d.shape[0], 1, width), glu)


def _t5_bucket_np(rel):
    max_exact = N_BUCKETS // 2
    n = np.maximum(rel, 0)
    nf = np.maximum(n, max_exact).astype(np.float32)
    large = max_exact + (np.log(nf / np.float32(max_exact)) / np.float32(math.log(MAX_DISTANCE / max_exact))
                         * np.float32(N_BUCKETS - max_exact)).astype(np.int32)
    large = np.minimum(large, N_BUCKETS - 1)
    return np.where(n < max_exact, n, large).astype(np.int32)


def _swa_table_kernel(bias_ref, sink_ref, bucket_ref, valid_ref, add_ref):
    h = pl.program_id(1)
    bucket = bucket_ref[...]
    valid = valid_ref[0] != 0
    acc = jnp.full(bucket.shape, NEG_INF, F32)
    for b in range(N_BUCKETS):
        acc = jnp.where(bucket == b, bias_ref[b, h], acc)
    acc = jnp.where(valid, acc, NEG_INF)
    col = lax.broadcasted_iota(jnp.int32, bucket.shape, 1)
    add_ref[0, 0] = jnp.where(col == 0, sink_ref[h], acc)


def _swa_table(rel_bias, sinks):
    blk = ATT_BLOCK
    qi = np.arange(blk)[:, None]
    kj = np.arange(2 * blk)[None, :]
    rel = qi + blk - kj
    in_window = (rel >= 0) & (rel < blk)
    valid = np.stack([in_window & (kj >= blk), in_window]).astype(np.int32)
    bucket = _t5_bucket_np(rel)
    return pl.pallas_call(
        _swa_table_kernel,
        grid=(2, N_Q_HEADS),
        in_specs=[pl.BlockSpec(memory_space=pltpu.SMEM),
                  pl.BlockSpec(memory_space=pltpu.SMEM),
                  pl.BlockSpec((blk, 2 * blk), lambda v, h: (0, 0)),
                  pl.BlockSpec((1, blk, 2 * blk), lambda v, h: (v, 0, 0))],
        out_specs=pl.BlockSpec((1, 1, blk, 2 * blk), lambda v, h: (v, h, 0, 0)),
        out_shape=jax.ShapeDtypeStruct((2, N_Q_HEADS, blk, 2 * blk), F32),
        compiler_params=_params("arbitrary", "arbitrary"),
        name="swa_table",
    )(rel_bias, sinks, jnp.asarray(bucket), jnp.asarray(valid))


def _swa_kernel(q_ref, kp_ref, kc_ref, vp_ref, vc_ref, add_ref, o_ref):
    blk, hd, g = ATT_BLOCK, HEAD_DIM, Q_PER_KV
    scale = HEAD_DIM ** -0.5
    assert math.frexp(scale)[0] == 0.5
    first_row = lax.broadcasted_iota(jnp.int32, (2 * blk, hd), 0) == 0
    outs = []
    for k in range(N_KV_HEADS):
        sl = slice(k * hd, (k + 1) * hd)
        keys = jnp.concatenate([kp_ref[:, sl], kc_ref[:, sl]], axis=0)
        vals = jnp.concatenate([vp_ref[:, sl], vc_ref[:, sl]], axis=0)
        keys = jnp.where(first_row, jnp.zeros_like(keys), keys)
        vals = jnp.where(first_row, jnp.zeros_like(vals), vals)
        q = jnp.concatenate([q_ref[:, (k * g + i) * hd:(k * g + i + 1) * hd] for i in range(g)],
                            axis=0) * scale
        s = lax.dot_general(q, keys, (((1,), (1,)), ((), ())), preferred_element_type=F32)
        s = s + add_ref[0, k * g:(k + 1) * g].reshape(g * blk, 2 * blk)
        m = jnp.max(s, axis=-1, keepdims=True)
        p = jnp.exp(s - m)
        den = jnp.sum(p, axis=-1, keepdims=True)
        o = jnp.dot(p.astype(BF16), vals, preferred_element_type=F32) / den
        outs.extend(o[i * blk:(i + 1) * blk] for i in range(g))
    o_ref[...] = jnp.concatenate(outs, axis=-1).astype(o_ref.dtype)


def _swa(z, add, batch, seq_len):
    blk = ATT_BLOCK
    nblk = seq_len // blk
    M = z.shape[0]
    dq = N_Q_HEADS * HEAD_DIM
    dkv = N_KV_HEADS * HEAD_DIM
    kcol, vcol = dq // dkv, dq // dkv + 1
    cur = lambda c: pl.BlockSpec((blk, dkv), lambda b, n: (b * nblk + n, c))
    prev = lambda c: pl.BlockSpec((blk, dkv), lambda b, n: (b * nblk + jnp.maximum(n - 1, 0), c))
    return pl.pallas_call(
        _swa_kernel,
        grid=(batch, nblk),
        in_specs=[pl.BlockSpec((blk, dq), lambda b, n: (b * nblk + n, 0)),
                  prev(kcol), cur(kcol), prev(vcol), cur(vcol),
                  pl.BlockSpec((1, N_Q_HEADS, blk, 2 * blk), lambda b, n: (jnp.minimum(n, 1), 0, 0, 0))],
        out_specs=pl.BlockSpec((blk, dq), lambda b, n: (b * nblk + n, 0)),
        out_shape=jax.ShapeDtypeStruct((M, dq), BF16),
        compiler_params=_params("arbitrary", "arbitrary"),
        name="swa",
    )(z, z, z, z, z, add)


def _xattn_kernel(rc, hn_ref, h_ref, kv_ref, wq_ref, wo_ref, g_ref, hout_ref, hnout_ref):
    d = hn_ref.shape[1]
    hd = d // X_HEADS
    g = g_ref[...]
    for c in range(hn_ref.shape[0] // rc):
        rows = slice(c * rc, (c + 1) * rc)
        q = jnp.dot(hn_ref[rows, :], wq_ref[...], preferred_element_type=F32).astype(BF16)
        outs = []
        for h in range(X_HEADS):
            k = kv_ref[:, h * hd:(h + 1) * hd]
            v = kv_ref[:, d + h * hd:d + (h + 1) * hd]
            s = lax.dot_general(q[:, h * hd:(h + 1) * hd], k, (((1,), (1,)), ((), ())),
                                preferred_element_type=F32) * (hd ** -0.5)
            m = jnp.max(s, axis=-1, keepdims=True)
            p = jnp.exp(s - m)
            den = jnp.sum(p, axis=-1, keepdims=True)
            outs.append((jnp.dot(p.astype(BF16), v, preferred_element_type=F32) / den).astype(BF16))
        o = jnp.concatenate(outs, axis=-1)
        acc = h_ref[rows, :] + jnp.dot(o, wo_ref[...], preferred_element_type=F32)
        hout_ref[rows, :] = acc
        hnout_ref[rows, :] = _rms(acc, g).astype(hnout_ref.dtype)


def _xattn_block(hn, h, kv, w_q, w_o, layer, g, batch, seq_len, n_mem, *, tq=512):
    M, d = hn.shape
    tq = min(tq, seq_len)
    rc = min(ROW_CHUNK, tq // 2)
    nq = seq_len // tq
    row = pl.BlockSpec((tq, d), lambda b, i: (b * nq + i, 0))
    wspec = pl.BlockSpec((None, d, d), lambda b, i: (layer, 0, 0), pipeline_mode=pl.Buffered(1))
    return pl.pallas_call(
        functools.partial(_xattn_kernel, rc),
        grid=(batch, nq),
        in_specs=[row, row,
                  pl.BlockSpec((n_mem, 2 * d), lambda b, i: (b, 0)),
                  wspec, wspec,
                  pl.BlockSpec((1, d), lambda b, i: (0, 0))],
        out_specs=[row, row],
        out_shape=[jax.ShapeDtypeStruct((M, d), F32), jax.ShapeDtypeStruct((M, d), BF16)],
        compiler_params=_params("arbitrary", "arbitrary"),
        name="xattn",
    )(hn, h, kv, w_q, w_o, g.reshape(1, d))


def kernel(x, mem, norm_mix, norm_xattn, norm_ffn, norm_final, norm_mem, rel_bias, ev_w_in, ev_conv_w, s5_a_re, s5_a_im, s5_log_dt, s5_b_re, s5_b_im, s5_c_re, s5_c_im, s5_d, s5_glu_w, ev_w_out, od_w_qkv, od_b_qkv, od_sinks, od_w_out, xa_w_q, xa_w_kv, xa_w_o, ff_w_gate, ff_w_up, ff_conv_w, ff_conv_b, ff_w_down):
    batch, seq_len, d = x.shape
    n_mem = mem.shape[1]
    depth = norm_mix.shape[0]
    M = batch * seq_len
    a_width = ev_conv_w.shape[-1]
    ev_w_out, od_w_out, xa_w_q, xa_w_o, ff_w_down = (
        w.astype(BF16) for w in (ev_w_out, od_w_out, xa_w_q, xa_w_o, ff_w_down))

    mem_n = _rmsnorm(mem.reshape(batch * n_mem, d), norm_mem)
    h = x.reshape(M, d)
    hn = _rmsnorm(h, norm_mix[0])
    for l in range(depth):
        i = l // 2
        if l % 2 == 0:
            ya = _mixer_a(hn, ev_w_in, i, ev_conv_w, seq_len)
            u = _mm(hn, ev_w_in, i, col0=3 * a_width, out_dtype=F32, name="s5_in")
            tables = _s5_tables(s5_a_re[i], s5_a_im[i], s5_log_dt[i], s5_b_re[i], s5_b_im[i],
                                s5_c_re[i], s5_c_im[i], s5_glu_w[i])
            ys = _s5(u, tables, s5_d, i, batch, seq_len)
            h, hn = _mm_res_norm([ya, ys], ev_w_out, i, h, norm_xattn[l], name="ev_out")
        else:
            z = _mm(hn, od_w_qkv, i, bias=od_b_qkv, tn=1280, name="qkv")
            o = _swa(z, _swa_table(rel_bias, od_sinks[i]), batch, seq_len)
            h, hn = _mm_res_norm([o], od_w_out, i, h, norm_xattn[l], name="od_out")
        kv = _mm(mem_n, xa_w_kv, l, name="xa_kv")
        h, hn = _xattn_block(hn, h, kv, xa_w_q, xa_w_o, l, norm_ffn[l], batch, seq_len, n_mem)
        act = _ffn_up(hn, ff_w_gate, ff_w_up, l, ff_conv_w, ff_conv_b, seq_len)
        if l + 1 < depth:
            h, hn = _mm_res_norm([act], ff_w_down, l, h, norm_mix[l + 1], tm=256, name="ff_down")
        else:
            _, out = _mm_res_norm([act], ff_w_down, l, h, norm_final, emit_h=False, norm_dtype=F32,
                                  tm=256, name="ff_down_final")
    return out.reshape(batch, seq_len, d)
```

```python
import functools
import math

import numpy as np
import jax
import jax.numpy as jnp
from jax import lax
from jax.experimental import pallas as pl
from jax.experimental.pallas import tpu as pltpu

F32 = jnp.float32
BF16 = jnp.bfloat16

RMS_EPS = 1e-5
NEG_INF = -1e30

HEAD_DIM = 64
Q_PER_KV = 8
N_KV_HEADS = 4
N_Q_HEADS = Q_PER_KV * N_KV_HEADS
ATT_BLOCK = 128
N_BUCKETS = 32
MAX_DISTANCE = 128
X_HEADS = 4
S5_GROUP = 16
S5_STATE = 64

LANES = 128
SUBLANES = 8
MXU_DIM = 256
VMEM_LIMIT_BYTES = 60000 * 1024

S5_CHUNK = 16
GROUPS_PER_TILE = LANES // S5_GROUP

ROW_CHUNK = 256


def _params(*sem):
    return pltpu.CompilerParams(dimension_semantics=sem, vmem_limit_bytes=VMEM_LIMIT_BYTES)


def _rms(x, g):
    ms = jnp.mean(x * x, axis=-1, keepdims=True)
    return x * lax.rsqrt(ms + RMS_EPS) * g


def _rmsnorm_kernel(x_ref, g_ref, o_ref):
    o_ref[...] = _rms(x_ref[...], g_ref[...]).astype(o_ref.dtype)


def _rmsnorm(x, g, *, tm=512):
    M, D = x.shape
    tm = min(tm, M)
    return pl.pallas_call(
        _rmsnorm_kernel,
        grid=(M // tm,),
        in_specs=[pl.BlockSpec((tm, D), lambda i: (i, 0)), pl.BlockSpec((1, D), lambda i: (0, 0))],
        out_specs=pl.BlockSpec((tm, D), lambda i: (i, 0)),
        out_shape=jax.ShapeDtypeStruct((M, D), BF16),
        compiler_params=_params("arbitrary"),
        name="rmsnorm",
    )(x, g.reshape(1, D))


def _mm_kernel(has_bias, rc, a_ref, w_ref, *rest):
    if has_bias:
        b_ref, o_ref, wbf_ref = rest
    else:
        o_ref, wbf_ref = rest

    @pl.when(pl.program_id(1) == 0)
    def _():
        wbf_ref[...] = w_ref[...].astype(BF16)

    for c in range(a_ref.shape[0] // rc):
        rows = slice(c * rc, (c + 1) * rc)
        acc = jnp.dot(a_ref[rows, :], wbf_ref[...], preferred_element_type=F32)
        if has_bias:
            acc = acc + b_ref[...]
        o_ref[rows, :] = acc.astype(o_ref.dtype)


def _mm(a, w, layer, *, col0=0, n=None, bias=None, out_dtype=BF16, tm=2048, tn=1024, name="mm"):
    M, K = a.shape
    n = w.shape[2] - col0 if n is None else n
    tm, tn = min(tm, M), min(tn, n)
    rc = min(2 * ROW_CHUNK, tm)
    assert M % tm == 0 and n % tn == 0 and col0 % tn == 0 and tm % rc == 0
    cb = col0 // tn
    in_specs = [pl.BlockSpec((tm, K), lambda j, i: (i, 0)),
                pl.BlockSpec((None, K, tn), lambda j, i: (layer, 0, cb + j))]
    args = [a, w]
    if bias is not None:
        in_specs.append(pl.BlockSpec((None, 1, tn), lambda j, i: (layer, 0, cb + j)))
        args.append(bias.reshape(bias.shape[0], 1, -1))
    return pl.pallas_call(
        functools.partial(_mm_kernel, bias is not None, rc),
        grid=(n // tn, M // tm),
        in_specs=in_specs,
        out_specs=pl.BlockSpec((tm, tn), lambda j, i: (i, j)),
        out_shape=jax.ShapeDtypeStruct((M, n), out_dtype),
        scratch_shapes=[pltpu.VMEM((K, tn), BF16)],
        compiler_params=_params("arbitrary", "arbitrary"),
        name=name,
    )(*args)


def _mm_res_norm_kernel(emit_h, rc, n_a, *refs):
    a_refs = refs[:n_a]
    w_ref, h_ref, g_ref = refs[n_a:n_a + 3]
    outs = refs[n_a + 3:]
    hn_ref = outs[-1]
    g = g_ref[...]
    for c in range(h_ref.shape[0] // rc):
        rows = slice(c * rc, (c + 1) * rc)
        acc = h_ref[rows, :]
        k0 = 0
        for a_ref in a_refs:
            k1 = k0 + a_ref.shape[1]
            acc = acc + jnp.dot(a_ref[rows, :], w_ref[k0:k1, :], preferred_element_type=F32)
            k0 = k1
        if emit_h:
            outs[0][rows, :] = acc
        hn_ref[rows, :] = _rms(acc, g).astype(hn_ref.dtype)


def _mm_res_norm(a_parts, w, layer, h, g, *, emit_h=True, norm_dtype=BF16, tm=512, name="mm_res_norm"):
    M, D = h.shape
    assert sum(a.shape[1] for a in a_parts) == w.shape[1]
    tm = min(tm, M)
    rc = min(ROW_CHUNK, tm // 2)
    assert M % tm == 0 and tm % rc == 0
    row = pl.BlockSpec((tm, D), lambda i: (i, 0))
    out_shape = [jax.ShapeDtypeStruct((M, D), norm_dtype)]
    out_specs = [row]
    if emit_h:
        out_shape.insert(0, jax.ShapeDtypeStruct((M, D), F32))
        out_specs.insert(0, row)
    res = pl.pallas_call(
        functools.partial(_mm_res_norm_kernel, emit_h, rc, len(a_parts)),
        grid=(M // tm,),
        in_specs=[pl.BlockSpec((tm, a.shape[1]), lambda i: (i, 0)) for a in a_parts] + [
            pl.BlockSpec((None,) + w.shape[1:], lambda i: (layer, 0, 0), pipeline_mode=pl.Buffered(1)),
            row,
            pl.BlockSpec((1, D), lambda i: (0, 0))],
        out_specs=out_specs,
        out_shape=out_shape,
        compiler_params=_params("arbitrary"),
        name=name,
    )(*a_parts, w, h, g.reshape(1, D))
    return res if emit_h else (None, res[0])


def _conv3_rows(p, prev8, w_ref):
    w0, w1, w2 = w_ref[0:1, :], w_ref[1:2, :], w_ref[2:3, :]
    full = w2 * p + w1 * pltpu.roll(p, 1, 0) + w0 * pltpu.roll(p, 2, 0)
    e = jnp.concatenate([prev8, p[0:SUBLANES]], axis=0)
    f = w2 * e + w1 * pltpu.roll(e, 1, 0) + w0 * pltpu.roll(e, 2, 0)
    return full, f[SUBLANES:2 * SUBLANES]


def _load_halo(halo_ref, tiles_per_seq):
    @pl.when(pl.program_id(1) % tiles_per_seq == 0)
    def _():
        halo_ref[...] = jnp.zeros_like(halo_ref)
    return halo_ref[...]


def _mixer_a_kernel(tiles_per_seq, rc, a_ref, wb_ref, wc_ref, wx_ref, cw_ref, o_ref,
                    wb_s, wc_s, wx_s, halo_s):
    @pl.when(pl.program_id(1) == 0)
    def _():
        wb_s[...] = wb_ref[...].astype(BF16)
        wc_s[...] = wc_ref[...].astype(BF16)
        wx_s[...] = wx_ref[...].astype(BF16)

    prev8 = _load_halo(halo_s, tiles_per_seq)
    for c in range(a_ref.shape[0] // rc):
        r0 = c * rc
        a = a_ref[r0:r0 + rc, :]
        gb = jnp.dot(a, wb_s[...], preferred_element_type=F32)
        p = (jnp.dot(a, wc_s[...], preferred_element_type=F32)
             * jnp.dot(a, wx_s[...], preferred_element_type=F32))
        full, first8 = _conv3_rows(p, prev8, cw_ref)
        prev8 = p[rc - SUBLANES:rc]
        o_ref[r0:r0 + rc, :] = (gb * full).astype(o_ref.dtype)
        o_ref[r0:r0 + SUBLANES, :] = (gb[0:SUBLANES] * first8).astype(o_ref.dtype)
    halo_s[...] = prev8


def _mixer_a(hn, w_in, layer, conv_w, seq_len, *, tm=2048, tn=512):
    M, K = hn.shape
    ca = conv_w.shape[2]
    tm, tn = min(tm, seq_len), min(tn, ca)
    rc = min(ROW_CHUNK, tm)
    assert seq_len % tm == 0 and ca % tn == 0 and tm % rc == 0
    nj = ca // tn
    wspec = lambda off: pl.BlockSpec((None, K, tn), lambda j, i: (layer, 0, off * nj + j))
    return pl.pallas_call(
        functools.partial(_mixer_a_kernel, seq_len // tm, rc),
        grid=(nj, M // tm),
        in_specs=[pl.BlockSpec((tm, K), lambda j, i: (i, 0)),
                  wspec(0), wspec(1), wspec(2),
                  pl.BlockSpec((None, 3, tn), lambda j, i: (layer, 0, j))],
        out_specs=pl.BlockSpec((tm, tn), lambda j, i: (i, j)),
        out_shape=jax.ShapeDtypeStruct((M, ca), BF16),
        scratch_shapes=[pltpu.VMEM((K, tn), BF16)] * 3 + [pltpu.VMEM((SUBLANES, tn), F32)],
        compiler_params=_params("arbitrary", "arbitrary"),
        name="mixer_a",
    )(hn, w_in, w_in, w_in, conv_w)


def _ffn_up_kernel(tiles_per_seq, rc, a_ref, wg_ref, wu_ref, cw_ref, cb_ref, wd_ref, o_ref, wd_bf_ref,
                   wg_s, wu_s, halo_s):
    @pl.when(pl.program_id(1) == 0)
    def _():
        wg_s[...] = wg_ref[...].astype(BF16)
        wu_s[...] = wu_ref[...].astype(BF16)
        wd_bf_ref[...] = wd_ref[...].astype(BF16)

    b = cb_ref[...]
    prev8 = _load_halo(halo_s, tiles_per_seq)
    for c in range(a_ref.shape[0] // rc):
        r0 = c * rc
        a = a_ref[r0:r0 + rc, :]
        gate = jnp.dot(a, wg_s[...], preferred_element_type=F32)
        up = jnp.dot(a, wu_s[...], preferred_element_type=F32)
        full, first8 = _conv3_rows(gate, prev8, cw_ref)
        prev8 = gate[rc - SUBLANES:rc]
        o_ref[r0:r0 + rc, :] = (jax.nn.silu(full + b) * up).astype(o_ref.dtype)
        o_ref[r0:r0 + SUBLANES, :] = (jax.nn.silu(first8 + b) * up[0:SUBLANES]).astype(o_ref.dtype)
    halo_s[...] = prev8


def _ffn_up(hn, w_gate, w_up, w_down, layer, conv_w, conv_b, seq_len, *, tm=2048, tn=512):
    M, K = hn.shape
    dff, d_out = w_down.shape[1:]
    tm, tn = min(tm, seq_len), min(tn, dff)
    rc = min(ROW_CHUNK, tm)
    assert seq_len % tm == 0 and dff % tn == 0 and tm % rc == 0
    wspec = pl.BlockSpec((None, K, tn), lambda j, i: (layer, 0, j))
    return pl.pallas_call(
        functools.partial(_ffn_up_kernel, seq_len // tm, rc),
        grid=(dff // tn, M // tm),
        in_specs=[pl.BlockSpec((tm, K), lambda j, i: (i, 0)),
                  wspec, wspec,
                  pl.BlockSpec((None, 3, tn), lambda j, i: (layer, 0, j)),
                  pl.BlockSpec((None, 1, tn), lambda j, i: (layer, 0, j)),
                  pl.BlockSpec((None, tn, d_out), lambda j, i: (layer, j, 0))],
        out_specs=[pl.BlockSpec((tm, tn), lambda j, i: (i, j)),
                   pl.BlockSpec((None, tn, d_out), lambda j, i: (0, j, 0))],
        out_shape=[jax.ShapeDtypeStruct((M, dff), BF16), jax.ShapeDtypeStruct((1, dff, d_out), BF16)],
        scratch_shapes=[pltpu.VMEM((K, tn), BF16)] * 2 + [pltpu.VMEM((SUBLANES, tn), F32)],
        compiler_params=_params("arbitrary", "arbitrary"),
        name="ffn_up",
    )(hn, w_gate, w_up, conv_w, conv_b.reshape(conv_b.shape[0], 1, dff), w_down)


def _log2(n):
    assert n & (n - 1) == 0
    return n.bit_length() - 1


def _s5_selectors(P, H):
    T, gt = S5_CHUNK, GROUPS_PER_TILE
    eye_h, eye_p = np.eye(H, dtype=np.float32), np.eye(P, dtype=np.float32)
    sel_v = np.kron(np.eye(2, dtype=np.float32), np.tile(eye_p, (1, gt)))
    sel_o = np.kron(np.eye(T, dtype=np.float32), np.tile(eye_h, (1, gt)))
    return jnp.asarray(sel_v, BF16), jnp.asarray(sel_o, BF16)


def _s5_tables(a_re, a_im, log_dt, b_re, b_im, c_re, c_im, glu_w):
    hp = lax.Precision.HIGHEST
    T, gt = S5_CHUNK, GROUPS_PER_TILE
    G, P = a_re.shape
    H = b_re.shape[-1]
    J = G // gt
    dt = jnp.exp(log_dt)[:, None]
    lr, li = a_re * dt, a_im * dt

    def a_pow(taus):
        tau = jnp.asarray(np.asarray(taus, np.float32))[:, None, None]
        mag = jnp.exp(tau * lr)
        return mag * jnp.cos(tau * li), mag * jnp.sin(tau * li)

    pw_re, pw_im = a_pow(np.arange(T + 1))
    nr, ni = pw_re[1] - 1.0, pw_im[1]
    den = a_re * a_re + a_im * a_im
    qr, qi = (nr * a_re + ni * a_im) / den, (ni * a_re - nr * a_im) / den
    bb_re = qr[..., None] * b_re - qi[..., None] * b_im
    bb_im = qr[..., None] * b_im + qi[..., None] * b_re
    cp_re = c_re[None] * pw_re[:, :, None, :] - c_im[None] * pw_im[:, :, None, :]
    cp_im = c_re[None] * pw_im[:, :, None, :] + c_im[None] * pw_re[:, :, None, :]
    kern = (jnp.einsum('tgop,gph->tgoh', cp_re[:T], bb_re, precision=hp)
            - jnp.einsum('tgop,gph->tgoh', cp_im[:T], bb_im, precision=hp))
    rev_re, rev_im = a_pow(T - 1 - np.arange(T))
    wv_re = rev_re[..., None] * bb_re[None] - rev_im[..., None] * bb_im[None]
    wv_im = rev_re[..., None] * bb_im[None] + rev_im[..., None] * bb_re[None]

    def tiles(compact):
        full = jnp.dot(compact.astype(BF16), jnp.asarray(np.tile(np.eye(H, dtype=np.float32), (1, gt)), BF16),
                       preferred_element_type=F32)
        rg = lax.broadcasted_iota(jnp.int32, (LANES, LANES), 0) // H
        cg = lax.broadcasted_iota(jnp.int32, (LANES, LANES), 1) // H
        return jnp.where(rg == cg, full, 0.0).astype(BF16)

    kc = kern.reshape(T, J, gt, H, H).transpose(1, 0, 2, 4, 3).reshape(J, T, gt * H, H)
    dtile = tiles(kc)
    glu = tiles(glu_w.reshape(J, gt * H, H))

    def in_rows(x):
        return x.reshape(T, J, gt, P, H).transpose(1, 0, 2, 4, 3).reshape(J, T * LANES, P)
    wvc = jnp.concatenate([in_rows(wv_re), in_rows(wv_im)], axis=-1).astype(BF16)

    def out_rows(x):
        return x.reshape(T, J, gt, H, P).transpose(1, 2, 4, 0, 3).reshape(J, gt * P, T * H)
    woc = jnp.concatenate([out_rows(cp_re[1:]), out_rows(-cp_im[1:])], axis=1).astype(BF16)
    a_chunk = jnp.concatenate([pw_re[T].reshape(J, 1, gt * P), pw_im[T].reshape(J, 1, gt * P)], axis=-1)
    return dtile, wvc, woc, a_chunk, glu


def _gelu_tanh(x):
    c = math.sqrt(2.0 / math.pi)
    return 0.5 * x * (1.0 + jnp.tanh(c * (x + 0.044715 * (x * x * x))))


def _expand_groups(compact_ref, sel_ref, out_ref, row_group, col_group, rows_per_step=512):
    for r0 in range(0, out_ref.shape[0], rows_per_step):
        full = jnp.dot(compact_ref[0, r0:r0 + rows_per_step, :], sel_ref[...], preferred_element_type=F32)
        r = lax.broadcasted_iota(jnp.int32, full.shape, 0) + r0
        c = lax.broadcasted_iota(jnp.int32, full.shape, 1)
        out_ref[r0:r0 + rows_per_step, :] = jnp.where(row_group(r) == col_group(c), full, 0.0).astype(BF16)


def _s5_kernel(u_ref, dt_ref, wvc_ref, woc_ref, selv_ref, selo_ref, ac_ref, d_ref, glu_ref, o_ref,
               mi_s, wv_s, wo_s, ubf_s, v_s, sp_s, y_s, ynat_s):
    T, gt = S5_CHUNK, GROUPS_PER_TILE
    nc = u_ref.shape[0] // T
    ns = ac_ref.shape[-1] // 2
    pair = MXU_DIM // LANES

    @pl.when(pl.program_id(1) == 0)
    def _():
        for s in range(T):
            for t in range(T):
                if t >= s:
                    tile = dt_ref[0, t - s]
                elif t // pair == s // pair:
                    tile = jnp.zeros((LANES, LANES), BF16)
                else:
                    continue
                mi_s[s * LANES:(s + 1) * LANES, t * LANES:(t + 1) * LANES] = tile
        lh, lp = _log2(S5_GROUP), _log2(ns // gt)
        _expand_groups(wvc_ref, selv_ref, wv_s,
                       lambda r: (r >> lh) & (gt - 1), lambda c: (c & (ns - 1)) >> lp)
        _expand_groups(woc_ref, selo_ref, wo_s,
                       lambda r: (r & (ns - 1)) >> lp, lambda c: (c & (LANES - 1)) >> lh)

    for t in range(T):
        ubf_s[:, t * LANES:(t + 1) * LANES] = u_ref[pl.ds(t, nc, stride=T), :].astype(BF16)
    v_s[...] = jnp.dot(ubf_s[...], wv_s[...], preferred_element_type=F32)
    ar, ai = ac_ref[0, :, 0:ns], ac_ref[0, :, ns:2 * ns]

    def step(c, carry):
        sr, si = carry
        sp_s[pl.ds(c, 1), 0:ns] = sr
        sp_s[pl.ds(c, 1), ns:2 * ns] = si
        vr = v_s[pl.ds(c, 1), 0:ns]
        vi = v_s[pl.ds(c, 1), ns:2 * ns]
        return ar * sr - ai * si + vr, ar * si + ai * sr + vi

    zero = jnp.zeros((1, ns), F32)
    lax.fori_loop(0, nc, step, (zero, zero))
    sp = sp_s[...].astype(BF16)
    for cb in range(T // pair):
        kk = (cb + 1) * MXU_DIM
        cols = slice(cb * MXU_DIM, (cb + 1) * MXU_DIM)
        y_s[:, cols] = (jnp.dot(ubf_s[:, 0:kk], mi_s[0:kk, cols], preferred_element_type=F32)
                        + jnp.dot(sp, wo_s[:, cols], preferred_element_type=F32))
    for t in range(T):
        ynat_s[pl.ds(t, nc, stride=T), :] = y_s[:, t * LANES:(t + 1) * LANES]
    y = ynat_s[...] + d_ref[...] * u_ref[...]
    yg = _gelu_tanh(y)
    gate = jnp.dot(yg.astype(BF16), glu_ref[0], preferred_element_type=F32)
    o_ref[...] = (yg * jax.nn.sigmoid(gate)).astype(o_ref.dtype)


def _s5(u, tables, selectors, d, layer, batch, seq_len):
    dtile, wvc, woc, a_chunk, glu = tables
    sel_v, sel_o = selectors
    J = wvc.shape[1]
    M, width = u.shape
    T = S5_CHUNK
    nc = seq_len // T
    n2 = a_chunk.shape[-1]
    per_tile = lambda x: pl.BlockSpec((None, 1) + x.shape[2:], lambda j, b: (layer, j) + (0,) * (x.ndim - 2))
    whole = lambda x: pl.BlockSpec(x.shape, lambda j, b: (0,) * x.ndim)
    return pl.pallas_call(
        _s5_kernel,
        grid=(J, batch),
        in_specs=[pl.BlockSpec((seq_len, LANES), lambda j, b: (b, j)),
                  per_tile(dtile), per_tile(wvc), per_tile(woc), whole(sel_v), whole(sel_o),
                  per_tile(a_chunk),
                  pl.BlockSpec((None, 1, LANES), lambda j, b: (layer, 0, j)),
                  per_tile(glu)],
        out_specs=pl.BlockSpec((seq_len, LANES), lambda j, b: (b, j)),
        out_shape=jax.ShapeDtypeStruct((M, width), BF16),
        scratch_shapes=[pltpu.VMEM((T * LANES, T * LANES), BF16),
                        pltpu.VMEM((T * LANES, n2), BF16),
                        pltpu.VMEM((n2, T * LANES), BF16),
                        pltpu.VMEM((nc, T * LANES), BF16),
                        pltpu.VMEM((nc, n2), F32),
                        pltpu.VMEM((nc, n2), F32),
                        pltpu.VMEM((nc, T * LANES), F32),
                        pltpu.VMEM((seq_len, LANES), F32)],
        compiler_params=_params("arbitrary", "arbitrary"),
        name="s5",
    )(u, dtile, wvc, woc, sel_v, sel_o, a_chunk, d.reshape(d.shape[0], 1, width), glu)


def _t5_bucket_np(rel):
    max_exact = N_BUCKETS // 2
    n = np.maximum(rel, 0)
    nf = np.maximum(n, max_exact).astype(np.float32)
    large = max_exact + (np.log(nf / np.float32(max_exact)) / np.float32(math.log(MAX_DISTANCE / max_exact))
                         * np.float32(N_BUCKETS - max_exact)).astype(np.int32)
    large = np.minimum(large, N_BUCKETS - 1)
    return np.where(n < max_exact, n, large).astype(np.int32)


def _swa_table_kernel(bias_ref, sink_ref, bucket_ref, valid_ref, add_ref):
    h = pl.program_id(1)
    bucket = bucket_ref[...]
    valid = valid_ref[0] != 0
    acc = jnp.full(bucket.shape, NEG_INF, F32)
    for b in range(N_BUCKETS):
        acc = jnp.where(bucket == b, bias_ref[b, h], acc)
    acc = jnp.where(valid, acc, NEG_INF)
    col = lax.broadcasted_iota(jnp.int32, bucket.shape, 1)
    add_ref[0, 0] = jnp.where(col == 0, sink_ref[h], acc)


def _swa_table(rel_bias, sinks):
    blk = ATT_BLOCK
    qi = np.arange(blk)[:, None]
    kj = np.arange(2 * blk)[None, :]
    rel = qi + blk - kj
    in_window = (rel >= 0) & (rel < blk)
    valid = np.stack([in_window & (kj >= blk), in_window]).astype(np.int32)
    bucket = _t5_bucket_np(rel)
    return pl.pallas_call(
        _swa_table_kernel,
        grid=(2, N_Q_HEADS),
        in_specs=[pl.BlockSpec(memory_space=pltpu.SMEM),
                  pl.BlockSpec(memory_space=pltpu.SMEM),
                  pl.BlockSpec((blk, 2 * blk), lambda v, h: (0, 0)),
                  pl.BlockSpec((1, blk, 2 * blk), lambda v, h: (v, 0, 0))],
        out_specs=pl.BlockSpec((1, 1, blk, 2 * blk), lambda v, h: (v, h, 0, 0)),
        out_shape=jax.ShapeDtypeStruct((2, N_Q_HEADS, blk, 2 * blk), F32),
        compiler_params=_params("arbitrary", "arbitrary"),
        name="swa_table",
    )(rel_bias, sinks, jnp.asarray(bucket), jnp.asarray(valid))


def _swa_kernel(q_ref, kp_ref, kc_ref, vp_ref, vc_ref, add_ref, o_ref):
    blk, hd, g = ATT_BLOCK, HEAD_DIM, Q_PER_KV
    scale = HEAD_DIM ** -0.5
    assert math.frexp(scale)[0] == 0.5
    first_row = lax.broadcasted_iota(jnp.int32, (2 * blk, hd), 0) == 0
    outs = []
    for k in range(N_KV_HEADS):
        sl = slice(k * hd, (k + 1) * hd)
        keys = jnp.concatenate([kp_ref[:, sl], kc_ref[:, sl]], axis=0)
        vals = jnp.concatenate([vp_ref[:, sl], vc_ref[:, sl]], axis=0)
        keys = jnp.where(first_row, jnp.zeros_like(keys), keys)
        vals = jnp.where(first_row, jnp.zeros_like(vals), vals)
        q = jnp.concatenate([q_ref[:, (k * g + i) * hd:(k * g + i + 1) * hd] for i in range(g)],
                            axis=0) * scale
        s = lax.dot_general(q, keys, (((1,), (1,)), ((), ())), preferred_element_type=F32)
        s = s + add_ref[0, k * g:(k + 1) * g].reshape(g * blk, 2 * blk)
        m = jnp.max(s, axis=-1, keepdims=True)
        p = jnp.exp(s - m)
        den = jnp.sum(p, axis=-1, keepdims=True)
        o = jnp.dot(p.astype(BF16), vals, preferred_element_type=F32) / den
        outs.extend(o[i * blk:(i + 1) * blk] for i in range(g))
    o_ref[...] = jnp.concatenate(outs, axis=-1).astype(o_ref.dtype)


def _swa(z, add, batch, seq_len):
    blk = ATT_BLOCK
    nblk = seq_len // blk
    M = z.shape[0]
    dq = N_Q_HEADS * HEAD_DIM
    dkv = N_KV_HEADS * HEAD_DIM
    kcol, vcol = dq // dkv, dq // dkv + 1
    cur = lambda c: pl.BlockSpec((blk, dkv), lambda b, n: (b * nblk + n, c))
    prev = lambda c: pl.BlockSpec((blk, dkv), lambda b, n: (b * nblk + jnp.maximum(n - 1, 0), c))
    return pl.pallas_call(
        _swa_kernel,
        grid=(batch, nblk),
        in_specs=[pl.BlockSpec((blk, dq), lambda b, n: (b * nblk + n, 0)),
                  prev(kcol), cur(kcol), prev(vcol), cur(vcol),
                  pl.BlockSpec((1, N_Q_HEADS, blk, 2 * blk), lambda b, n: (jnp.minimum(n, 1), 0, 0, 0))],
        out_specs=pl.BlockSpec((blk, dq), lambda b, n: (b * nblk + n, 0)),
        out_shape=jax.ShapeDtypeStruct((M, dq), BF16),
        compiler_params=_params("arbitrary", "arbitrary"),
        name="swa",
    )(z, z, z, z, z, add)


def _xattn_kernel(rc, hn_ref, h_ref, kv_ref, wq_ref, wo_ref, g_ref, hout_ref, hnout_ref):
    d = hn_ref.shape[1]
    hd = d // X_HEADS
    g = g_ref[...]
    for c in range(hn_ref.shape[0] // rc):
        rows = slice(c * rc, (c + 1) * rc)
        q = jnp.dot(hn_ref[rows, :], wq_ref[...], preferred_element_type=F32).astype(BF16)
        outs = []
        for h in range(X_HEADS):
            k = kv_ref[:, h * hd:(h + 1) * hd]
            v = kv_ref[:, d + h * hd:d + (h + 1) * hd]
            s = lax.dot_general(q[:, h * hd:(h + 1) * hd], k, (((1,), (1,)), ((), ())),
                                preferred_element_type=F32) * (hd ** -0.5)
            m = jnp.max(s, axis=-1, keepdims=True)
            p = jnp.exp(s - m)
            den = jnp.sum(p, axis=-1, keepdims=True)
            outs.append((jnp.dot(p.astype(BF16), v, preferred_element_type=F32) / den).astype(BF16))
        o = jnp.concatenate(outs, axis=-1)
        acc = h_ref[rows, :] + jnp.dot(o, wo_ref[...], preferred_element_type=F32)
        hout_ref[rows, :] = acc
        hnout_ref[rows, :] = _rms(acc, g).astype(hnout_ref.dtype)


def _xattn_block(hn, h, kv, w_q, w_o, layer, g, batch, seq_len, n_mem, *, tq=512):
    M, d = hn.shape
    tq = min(tq, seq_len)
    rc = min(ROW_CHUNK, tq // 2)
    nq = seq_len // tq
    row = pl.BlockSpec((tq, d), lambda b, i: (b * nq + i, 0))
    wspec = pl.BlockSpec((None, d, d), lambda b, i: (layer, 0, 0), pipeline_mode=pl.Buffered(1))
    return pl.pallas_call(
        functools.partial(_xattn_kernel, rc),
        grid=(batch, nq),
        in_specs=[row, row,
                  pl.BlockSpec((n_mem, 2 * d), lambda b, i: (b, 0)),
                  wspec, wspec,
                  pl.BlockSpec((1, d), lambda b, i: (0, 0))],
        out_specs=[row, row],
        out_shape=[jax.ShapeDtypeStruct((M, d), F32), jax.ShapeDtypeStruct((M, d), BF16)],
        compiler_params=_params("arbitrary", "arbitrary"),
        name="xattn",
    )(hn, h, kv, w_q, w_o, g.reshape(1, d))


def kernel(x, mem, norm_mix, norm_xattn, norm_ffn, norm_final, norm_mem, rel_bias, ev_w_in, ev_conv_w, s5_a_re, s5_a_im, s5_log_dt, s5_b_re, s5_b_im, s5_c_re, s5_c_im, s5_d, s5_glu_w, ev_w_out, od_w_qkv, od_b_qkv, od_sinks, od_w_out, xa_w_q, xa_w_kv, xa_w_o, ff_w_gate, ff_w_up, ff_conv_w, ff_conv_b, ff_w_down):
    batch, seq_len, d = x.shape
    n_mem = mem.shape[1]
    depth = norm_mix.shape[0]
    M = batch * seq_len
    a_width = ev_conv_w.shape[-1]
    ev_w_out, od_w_out, xa_w_q, xa_w_o = (w.astype(BF16) for w in (ev_w_out, od_w_out, xa_w_q, xa_w_o))

    s5_tables = jax.vmap(_s5_tables)(s5_a_re, s5_a_im, s5_log_dt, s5_b_re, s5_b_im, s5_c_re, s5_c_im, s5_glu_w)
    s5_selectors = _s5_selectors(s5_a_re.shape[-1], s5_b_re.shape[-1])

    mem_n = _rmsnorm(mem.reshape(batch * n_mem, d), norm_mem)
    h = x.reshape(M, d)
    hn = _rmsnorm(h, norm_mix[0])
    for l in range(depth):
        i = l // 2
        if l % 2 == 0:
            ya = _mixer_a(hn, ev_w_in, i, ev_conv_w, seq_len)
            u = _mm(hn, ev_w_in, i, col0=3 * a_width, out_dtype=F32, name="s5_in")
            ys = _s5(u, s5_tables, s5_selectors, s5_d, i, batch, seq_len)
            h, hn = _mm_res_norm([ya, ys], ev_w_out, i, h, norm_xattn[l], name="ev_out")
        else:
            z = _mm(hn, od_w_qkv, i, bias=od_b_qkv, tn=1280, name="qkv")
            o = _swa(z, _swa_table(rel_bias, od_sinks[i]), batch, seq_len)
            h, hn = _mm_res_norm([o], od_w_out, i, h, norm_xattn[l], name="od_out")
        kv = _mm(mem_n, xa_w_kv, l, name="xa_kv")
        h, hn = _xattn_block(hn, h, kv, xa_w_q, xa_w_o, l, norm_ffn[l], batch, seq_len, n_mem)
        act, w_down = _ffn_up(hn, ff_w_gate, ff_w_up, ff_w_down, l, ff_conv_w, ff_conv_b, seq_len)
        if l + 1 < depth:
            h, hn = _mm_res_norm([act], w_down, 0, h, norm_mix[l + 1], tm=256, name="ff_down")
        else:
            _, out = _mm_res_norm([act], w_down, 0, h, norm_final, emit_h=False, norm_dtype=F32,
                                  tm=256, name="ff_down_final")
    return out.reshape(batch, seq_len, d)
```

```python
import functools
import math

import numpy as np
import jax
import jax.numpy as jnp
from jax import lax
from jax.experimental import pallas as pl
from jax.experimental.pallas import tpu as pltpu

F32 = jnp.float32
BF16 = jnp.bfloat16

RMS_EPS = 1e-5
NEG_INF = -1e30

HEAD_DIM = 64
Q_PER_KV = 8
N_KV_HEADS = 4
N_Q_HEADS = Q_PER_KV * N_KV_HEADS
ATT_BLOCK = 128
N_BUCKETS = 32
MAX_DISTANCE = 128
X_HEADS = 4
S5_GROUP = 16
S5_STATE = 64

LANES = 128
SUBLANES = 8
MXU_DIM = 256
VMEM_LIMIT_BYTES = 60000 * 1024

S5_CHUNK = 16
GROUPS_PER_TILE = LANES // S5_GROUP

ROW_CHUNK = 256


def _params(*sem):
    return pltpu.CompilerParams(dimension_semantics=sem, vmem_limit_bytes=VMEM_LIMIT_BYTES)


def _rms(x, g):
    ms = jnp.mean(x * x, axis=-1, keepdims=True)
    return x * lax.rsqrt(ms + RMS_EPS) * g


def _rmsnorm_kernel(x_ref, g_ref, o_ref):
    o_ref[...] = _rms(x_ref[...], g_ref[...]).astype(o_ref.dtype)


def _rmsnorm(x, g, *, tm=512):
    M, D = x.shape
    tm = min(tm, M)
    return pl.pallas_call(
        _rmsnorm_kernel,
        grid=(M // tm,),
        in_specs=[pl.BlockSpec((tm, D), lambda i: (i, 0)), pl.BlockSpec((1, D), lambda i: (0, 0))],
        out_specs=pl.BlockSpec((tm, D), lambda i: (i, 0)),
        out_shape=jax.ShapeDtypeStruct((M, D), BF16),
        compiler_params=_params("arbitrary"),
        name="rmsnorm",
    )(x, g.reshape(1, D))


def _mm_kernel(has_bias, rc, a_ref, w_ref, *rest):
    if has_bias:
        b_ref, o_ref, wbf_ref = rest
    else:
        o_ref, wbf_ref = rest

    @pl.when(pl.program_id(1) == 0)
    def _():
        wbf_ref[...] = w_ref[...].astype(BF16)

    for c in range(a_ref.shape[0] // rc):
        rows = slice(c * rc, (c + 1) * rc)
        acc = jnp.dot(a_ref[rows, :], wbf_ref[...], preferred_element_type=F32)
        if has_bias:
            acc = acc + b_ref[...]
        o_ref[rows, :] = acc.astype(o_ref.dtype)


def _mm(a, w, layer, *, col0=0, n=None, bias=None, out_dtype=BF16, tm=2048, tn=1024, name="mm"):
    M, K = a.shape
    n = w.shape[2] - col0 if n is None else n
    tm, tn = min(tm, M), min(tn, n)
    rc = min(2 * ROW_CHUNK, tm)
    assert M % tm == 0 and n % tn == 0 and col0 % tn == 0 and tm % rc == 0
    cb = col0 // tn
    in_specs = [pl.BlockSpec((tm, K), lambda j, i: (i, 0)),
                pl.BlockSpec((None, K, tn), lambda j, i: (layer, 0, cb + j))]
    args = [a, w]
    if bias is not None:
        in_specs.append(pl.BlockSpec((None, 1, tn), lambda j, i: (layer, 0, cb + j)))
        args.append(bias.reshape(bias.shape[0], 1, -1))
    return pl.pallas_call(
        functools.partial(_mm_kernel, bias is not None, rc),
        grid=(n // tn, M // tm),
        in_specs=in_specs,
        out_specs=pl.BlockSpec((tm, tn), lambda j, i: (i, j)),
        out_shape=jax.ShapeDtypeStruct((M, n), out_dtype),
        scratch_shapes=[pltpu.VMEM((K, tn), BF16)],
        compiler_params=_params("arbitrary", "arbitrary"),
        name=name,
    )(*args)


def _mm_res_norm_kernel(emit_h, rc, n_a, *refs):
    a_refs = refs[:n_a]
    w_ref, h_ref, g_ref = refs[n_a:n_a + 3]
    outs = refs[n_a + 3:]
    hn_ref = outs[-1]
    g = g_ref[...]
    for c in range(h_ref.shape[0] // rc):
        rows = slice(c * rc, (c + 1) * rc)
        acc = h_ref[rows, :]
        k0 = 0
        for a_ref in a_refs:
            k1 = k0 + a_ref.shape[1]
            acc = acc + jnp.dot(a_ref[rows, :], w_ref[k0:k1, :], preferred_element_type=F32)
            k0 = k1
        if emit_h:
            outs[0][rows, :] = acc
        hn_ref[rows, :] = _rms(acc, g).astype(hn_ref.dtype)


def _mm_res_norm(a_parts, w, layer, h, g, *, emit_h=True, norm_dtype=BF16, tm=512, name="mm_res_norm"):
    M, D = h.shape
    assert sum(a.shape[1] for a in a_parts) == w.shape[1]
    tm = min(tm, M)
    rc = min(ROW_CHUNK, tm // 2)
    assert M % tm == 0 and tm % rc == 0
    row = pl.BlockSpec((tm, D), lambda i: (i, 0))
    out_shape = [jax.ShapeDtypeStruct((M, D), norm_dtype)]
    out_specs = [row]
    if emit_h:
        out_shape.insert(0, jax.ShapeDtypeStruct((M, D), F32))
        out_specs.insert(0, row)
    res = pl.pallas_call(
        functools.partial(_mm_res_norm_kernel, emit_h, rc, len(a_parts)),
        grid=(M // tm,),
        in_specs=[pl.BlockSpec((tm, a.shape[1]), lambda i: (i, 0)) for a in a_parts] + [
            pl.BlockSpec((None,) + w.shape[1:], lambda i: (layer, 0, 0), pipeline_mode=pl.Buffered(1)),
            row,
            pl.BlockSpec((1, D), lambda i: (0, 0))],
        out_specs=out_specs,
        out_shape=out_shape,
        compiler_params=_params("arbitrary"),
        name=name,
    )(*a_parts, w, h, g.reshape(1, D))
    return res if emit_h else (None, res[0])


def _conv3_rows(p, prev8, w_ref):
    w0, w1, w2 = w_ref[0:1, :], w_ref[1:2, :], w_ref[2:3, :]
    full = w2 * p + w1 * pltpu.roll(p, 1, 0) + w0 * pltpu.roll(p, 2, 0)
    e = jnp.concatenate([prev8, p[0:SUBLANES]], axis=0)
    f = w2 * e + w1 * pltpu.roll(e, 1, 0) + w0 * pltpu.roll(e, 2, 0)
    return full, f[SUBLANES:2 * SUBLANES]


def _load_halo(halo_ref, tiles_per_seq):
    @pl.when(pl.program_id(1) % tiles_per_seq == 0)
    def _():
        halo_ref[...] = jnp.zeros_like(halo_ref)
    return halo_ref[...]


def _mixer_a_kernel(tiles_per_seq, rc, a_ref, wb_ref, wc_ref, wx_ref, cw_ref, o_ref,
                    wb_s, wc_s, wx_s, halo_s):
    @pl.when(pl.program_id(1) == 0)
    def _():
        wb_s[...] = wb_ref[...].astype(BF16)
        wc_s[...] = wc_ref[...].astype(BF16)
        wx_s[...] = wx_ref[...].astype(BF16)

    prev8 = _load_halo(halo_s, tiles_per_seq)
    for c in range(a_ref.shape[0] // rc):
        r0 = c * rc
        a = a_ref[r0:r0 + rc, :]
        gb = jnp.dot(a, wb_s[...], preferred_element_type=F32)
        p = (jnp.dot(a, wc_s[...], preferred_element_type=F32)
             * jnp.dot(a, wx_s[...], preferred_element_type=F32))
        full, first8 = _conv3_rows(p, prev8, cw_ref)
        prev8 = p[rc - SUBLANES:rc]
        o_ref[r0:r0 + rc, :] = (gb * full).astype(o_ref.dtype)
        o_ref[r0:r0 + SUBLANES, :] = (gb[0:SUBLANES] * first8).astype(o_ref.dtype)
    halo_s[...] = prev8


def _mixer_a(hn, w_in, layer, conv_w, seq_len, *, tm=2048, tn=512):
    M, K = hn.shape
    ca = conv_w.shape[2]
    tm, tn = min(tm, seq_len), min(tn, ca)
    rc = min(ROW_CHUNK, tm)
    assert seq_len % tm == 0 and ca % tn == 0 and tm % rc == 0
    nj = ca // tn
    wspec = lambda off: pl.BlockSpec((None, K, tn), lambda j, i: (layer, 0, off * nj + j))
    return pl.pallas_call(
        functools.partial(_mixer_a_kernel, seq_len // tm, rc),
        grid=(nj, M // tm),
        in_specs=[pl.BlockSpec((tm, K), lambda j, i: (i, 0)),
                  wspec(0), wspec(1), wspec(2),
                  pl.BlockSpec((None, 3, tn), lambda j, i: (layer, 0, j))],
        out_specs=pl.BlockSpec((tm, tn), lambda j, i: (i, j)),
        out_shape=jax.ShapeDtypeStruct((M, ca), BF16),
        scratch_shapes=[pltpu.VMEM((K, tn), BF16)] * 3 + [pltpu.VMEM((SUBLANES, tn), F32)],
        compiler_params=_params("arbitrary", "arbitrary"),
        name="mixer_a",
    )(hn, w_in, w_in, w_in, conv_w)


def _ffn_up_kernel(tiles_per_seq, rc, a_ref, wg_ref, wu_ref, cw_ref, cb_ref, wd_ref, o_ref, wd_bf_ref,
                   wg_s, wu_s, halo_s):
    @pl.when(pl.program_id(1) == 0)
    def _():
        wg_s[...] = wg_ref[...].astype(BF16)
        wu_s[...] = wu_ref[...].astype(BF16)
        wd_bf_ref[...] = wd_ref[...].astype(BF16)

    b = cb_ref[...]
    prev8 = _load_halo(halo_s, tiles_per_seq)
    for c in range(a_ref.shape[0] // rc):
        r0 = c * rc
        a = a_ref[r0:r0 + rc, :]
        gate = jnp.dot(a, wg_s[...], preferred_element_type=F32)
        up = jnp.dot(a, wu_s[...], preferred_element_type=F32)
        full, first8 = _conv3_rows(gate, prev8, cw_ref)
        prev8 = gate[rc - SUBLANES:rc]
        o_ref[r0:r0 + rc, :] = (jax.nn.silu(full + b) * up).astype(o_ref.dtype)
        o_ref[r0:r0 + SUBLANES, :] = (jax.nn.silu(first8 + b) * up[0:SUBLANES]).astype(o_ref.dtype)
    halo_s[...] = prev8


def _ffn_up(hn, w_gate, w_up, w_down, layer, conv_w, conv_b, seq_len, *, tm=2048, tn=512):
    M, K = hn.shape
    dff, d_out = w_down.shape[1:]
    tm, tn = min(tm, seq_len), min(tn, dff)
    rc = min(ROW_CHUNK, tm)
    assert seq_len % tm == 0 and dff % tn == 0 and tm % rc == 0
    wspec = pl.BlockSpec((None, K, tn), lambda j, i: (layer, 0, j))
    return pl.pallas_call(
        functools.partial(_ffn_up_kernel, seq_len // tm, rc),
        grid=(dff // tn, M // tm),
        in_specs=[pl.BlockSpec((tm, K), lambda j, i: (i, 0)),
                  wspec, wspec,
                  pl.BlockSpec((None, 3, tn), lambda j, i: (layer, 0, j)),
                  pl.BlockSpec((None, 1, tn), lambda j, i: (layer, 0, j)),
                  pl.BlockSpec((None, tn, d_out), lambda j, i: (layer, j, 0))],
        out_specs=[pl.BlockSpec((tm, tn), lambda j, i: (i, j)),
                   pl.BlockSpec((None, tn, d_out), lambda j, i: (0, j, 0))],
        out_shape=[jax.ShapeDtypeStruct((M, dff), BF16), jax.ShapeDtypeStruct((1, dff, d_out), BF16)],
        scratch_shapes=[pltpu.VMEM((K, tn), BF16)] * 2 + [pltpu.VMEM((SUBLANES, tn), F32)],
        compiler_params=_params("arbitrary", "arbitrary"),
        name="ffn_up",
    )(hn, w_gate, w_up, conv_w, conv_b.reshape(conv_b.shape[0], 1, dff), w_down)


def _log2(n):
    assert n & (n - 1) == 0
    return n.bit_length() - 1


def _s5_selector(P):
    sel = np.kron(np.eye(2, dtype=np.float32), np.tile(np.eye(P, dtype=np.float32), (1, GROUPS_PER_TILE)))
    return jnp.asarray(sel, BF16)


def _s5_tables(a_re, a_im, log_dt, b_re, b_im, c_re, c_im, glu_w):
    T, gt = S5_CHUNK, GROUPS_PER_TILE
    G, P = a_re.shape
    H = b_re.shape[-1]
    J = G // gt
    dt = jnp.exp(log_dt)[:, None]
    lr, li = a_re * dt, a_im * dt

    def a_pow(taus):
        tau = jnp.asarray(np.asarray(taus, np.float32))[:, None, None]
        mag = jnp.exp(tau * lr)
        return lax.optimization_barrier((mag * jnp.cos(tau * li), mag * jnp.sin(tau * li)))

    pw_re, pw_im = a_pow(np.arange(T + 1))
    nr, ni = pw_re[1] - 1.0, pw_im[1]
    den = a_re * a_re + a_im * a_im
    qr, qi = (nr * a_re + ni * a_im) / den, (ni * a_re - nr * a_im) / den
    bt_re, bt_im = b_re.transpose(0, 2, 1), b_im.transpose(0, 2, 1)
    bb_re = qr[:, None, :] * bt_re - qi[:, None, :] * bt_im
    bb_im = qr[:, None, :] * bt_im + qi[:, None, :] * bt_re
    cp_re = c_re[None] * pw_re[:, :, None, :] - c_im[None] * pw_im[:, :, None, :]
    cp_im = c_re[None] * pw_im[:, :, None, :] + c_im[None] * pw_re[:, :, None, :]
    rev_re, rev_im = a_pow(T - 1 - np.arange(T))
    wv_re = rev_re[:, :, None, :] * bb_re[None] - rev_im[:, :, None, :] * bb_im[None]
    wv_im = rev_re[:, :, None, :] * bb_im[None] + rev_im[:, :, None, :] * bb_re[None]

    def tiles(compact):
        full = jnp.dot(compact.astype(BF16), jnp.asarray(np.tile(np.eye(H, dtype=np.float32), (1, gt)), BF16),
                       preferred_element_type=F32)
        rg = lax.broadcasted_iota(jnp.int32, (LANES, LANES), 0) // H
        cg = lax.broadcasted_iota(jnp.int32, (LANES, LANES), 1) // H
        return jnp.where(rg == cg, full, 0.0).astype(BF16)

    glu = tiles(glu_w.reshape(J, gt * H, H))

    wvc = jnp.concatenate([wv_re, wv_im], axis=-1).astype(BF16).reshape(T, J, gt * H, 2 * P)
    cpc = jnp.concatenate([cp_re, -cp_im], axis=-1).astype(BF16).reshape(T + 1, J, gt * H, 2 * P)
    a_chunk = jnp.concatenate([pw_re[T].reshape(J, 1, gt * P), pw_im[T].reshape(J, 1, gt * P)], axis=-1)
    return wvc, cpc, a_chunk, glu


def _gelu_tanh(x):
    c = math.sqrt(2.0 / math.pi)
    return 0.5 * x * (1.0 + jnp.tanh(c * (x + 0.044715 * (x * x * x))))


def _expand_groups(compact_ref, sel_ref, out_ref, steps_per_dot=4):
    nblk, rows, _ = compact_ref.shape
    lh, lp = _log2(S5_GROUP), _log2(out_ref.shape[1] // (2 * GROUPS_PER_TILE))
    for t0 in range(0, nblk, steps_per_dot):
        n = rows * min(steps_per_dot, nblk - t0)
        full = jnp.dot(compact_ref[t0:t0 + n // rows].reshape(n, -1), sel_ref[...],
                       preferred_element_type=F32)
        a = (lax.broadcasted_iota(jnp.int32, full.shape, 0) >> lh) & (GROUPS_PER_TILE - 1)
        b = (lax.broadcasted_iota(jnp.int32, full.shape, 1) >> lp) & (GROUPS_PER_TILE - 1)
        out_ref[t0 * rows:t0 * rows + n, :] = jnp.where(a == b, full, 0.0).astype(BF16)


def _s5_kernel(u_ref, wvc_ref, cpc_ref, sel_ref, ac_ref, d_ref, glu_ref, o_ref,
               mi_s, wv_s, cp_s, ubf_s, v_s, sp_s, y_s, ynat_s):
    T = S5_CHUNK
    nc = u_ref.shape[0] // T
    ns = ac_ref.shape[-1] // 2
    pair = MXU_DIM // LANES

    @pl.when(pl.program_id(1) == 0)
    def _():
        _expand_groups(wvc_ref, sel_ref, wv_s)
        _expand_groups(cpc_ref, sel_ref, cp_s)
        toe = lax.dot_general(wv_s[(T - 1) * LANES:T * LANES, :], cp_s[0:T * LANES, :],
                              (((1,), (1,)), ((), ())), preferred_element_type=F32).astype(BF16)
        for s in range(T):
            for t in range(T):
                if t >= s:
                    tile = toe[:, (t - s) * LANES:(t - s + 1) * LANES]
                elif t // pair == s // pair:
                    tile = jnp.zeros((LANES, LANES), BF16)
                else:
                    continue
                mi_s[s * LANES:(s + 1) * LANES, t * LANES:(t + 1) * LANES] = tile

    for t in range(T):
        ubf_s[:, t * LANES:(t + 1) * LANES] = u_ref[pl.ds(t, nc, stride=T), :].astype(BF16)
    v_s[...] = jnp.dot(ubf_s[...], wv_s[...], preferred_element_type=F32)
    ar, ai = ac_ref[0, :, 0:ns], ac_ref[0, :, ns:2 * ns]

    def step(c, carry):
        sr, si = carry
        sp_s[pl.ds(c, 1), 0:ns] = sr
        sp_s[pl.ds(c, 1), ns:2 * ns] = si
        vr = v_s[pl.ds(c, 1), 0:ns]
        vi = v_s[pl.ds(c, 1), ns:2 * ns]
        return ar * sr - ai * si + vr, ar * si + ai * sr + vi

    zero = jnp.zeros((1, ns), F32)
    lax.fori_loop(0, nc, step, (zero, zero))
    sp = sp_s[...].astype(BF16)
    for cb in range(T // pair):
        kk = (cb + 1) * MXU_DIM
        cols = slice(cb * MXU_DIM, (cb + 1) * MXU_DIM)
        wo_rows = slice(LANES + cb * MXU_DIM, LANES + (cb + 1) * MXU_DIM)
        y_s[:, cols] = (jnp.dot(ubf_s[:, 0:kk], mi_s[0:kk, cols], preferred_element_type=F32)
                        + lax.dot_general(sp, cp_s[wo_rows, :], (((1,), (1,)), ((), ())),
                                          preferred_element_type=F32))
    for t in range(T):
        ynat_s[pl.ds(t, nc, stride=T), :] = y_s[:, t * LANES:(t + 1) * LANES]
    y = ynat_s[...] + d_ref[...] * u_ref[...]
    yg = _gelu_tanh(y)
    gate = jnp.dot(yg.astype(BF16), glu_ref[0], preferred_element_type=F32)
    o_ref[...] = (yg * jax.nn.sigmoid(gate)).astype(o_ref.dtype)


def _s5(u, tables, sel, d, layer, batch, seq_len):
    wvc, cpc, a_chunk, glu = tables
    J = glu.shape[1]
    M, width = u.shape
    T = S5_CHUNK
    nc = seq_len // T
    n2 = a_chunk.shape[-1]
    per_tile = lambda x: pl.BlockSpec((None, 1) + x.shape[2:], lambda j, b: (layer, j) + (0,) * (x.ndim - 2))
    compact = lambda x: pl.BlockSpec((None, x.shape[1], None) + x.shape[3:], lambda j, b: (layer, 0, j, 0, 0))
    return pl.pallas_call(
        _s5_kernel,
        grid=(J, batch),
        in_specs=[pl.BlockSpec((seq_len, LANES), lambda j, b: (b, j)),
                  compact(wvc), compact(cpc),
                  pl.BlockSpec(sel.shape, lambda j, b: (0, 0)),
                  per_tile(a_chunk),
                  pl.BlockSpec((None, 1, LANES), lambda j, b: (layer, 0, j)),
                  per_tile(glu)],
        out_specs=pl.BlockSpec((seq_len, LANES), lambda j, b: (b, j)),
        out_shape=jax.ShapeDtypeStruct((M, width), BF16),
        scratch_shapes=[pltpu.VMEM((T * LANES, T * LANES), BF16),
                        pltpu.VMEM((T * LANES, n2), BF16),
                        pltpu.VMEM(((T + 1) * LANES, n2), BF16),
                        pltpu.VMEM((nc, T * LANES), BF16),
                        pltpu.VMEM((nc, n2), F32),
                        pltpu.VMEM((nc, n2), F32),
                        pltpu.VMEM((nc, T * LANES), F32),
                        pltpu.VMEM((seq_len, LANES), F32)],
        compiler_params=_params("arbitrary", "arbitrary"),
        name="s5",
    )(u, wvc, cpc, sel, a_chunk, d.reshape(d.shape[0], 1, width), glu)


def _t5_bucket_np(rel):
    max_exact = N_BUCKETS // 2
    n = np.maximum(rel, 0)
    nf = np.maximum(n, max_exact).astype(np.float32)
    large = max_exact + (np.log(nf / np.float32(max_exact)) / np.float32(math.log(MAX_DISTANCE / max_exact))
                         * np.float32(N_BUCKETS - max_exact)).astype(np.int32)
    large = np.minimum(large, N_BUCKETS - 1)
    return np.where(n < max_exact, n, large).astype(np.int32)


def _swa_table_kernel(bias_ref, sink_ref, bucket_ref, valid_ref, add_ref):
    bucket = bucket_ref[...]
    valid = valid_ref[0] != 0
    sink_col = lax.broadcasted_iota(jnp.int32, bucket.shape, 1) == 0

    @pl.loop(0, add_ref.shape[1])
    def _(h):
        acc = jnp.full(bucket.shape, NEG_INF, F32)
        for b in range(N_BUCKETS):
            acc = jnp.where(bucket == b, bias_ref[b, h], acc)
        acc = jnp.where(valid, acc, NEG_INF)
        add_ref[0, h] = jnp.where(sink_col, sink_ref[h], acc)


def _swa_table(rel_bias, sinks):
    blk = ATT_BLOCK
    qi = np.arange(blk)[:, None]
    kj = np.arange(2 * blk)[None, :]
    rel = qi + blk - kj
    in_window = (rel >= 0) & (rel < blk)
    valid = np.stack([in_window & (kj >= blk), in_window]).astype(np.int32)
    bucket = _t5_bucket_np(rel)
    return pl.pallas_call(
        _swa_table_kernel,
        grid=(2,),
        in_specs=[pl.BlockSpec(memory_space=pltpu.SMEM),
                  pl.BlockSpec(memory_space=pltpu.SMEM),
                  pl.BlockSpec((blk, 2 * blk), lambda v: (0, 0)),
                  pl.BlockSpec((1, blk, 2 * blk), lambda v: (v, 0, 0))],
        out_specs=pl.BlockSpec((1, N_Q_HEADS, blk, 2 * blk), lambda v: (v, 0, 0, 0)),
        out_shape=jax.ShapeDtypeStruct((2, N_Q_HEADS, blk, 2 * blk), F32),
        compiler_params=_params("arbitrary"),
        name="swa_table",
    )(rel_bias, sinks, jnp.asarray(bucket), jnp.asarray(valid))


def _swa_kernel(q_ref, kp_ref, kc_ref, vp_ref, vc_ref, add_ref, o_ref):
    blk, hd, g = ATT_BLOCK, HEAD_DIM, Q_PER_KV
    scale = HEAD_DIM ** -0.5
    assert math.frexp(scale)[0] == 0.5
    first_row = lax.broadcasted_iota(jnp.int32, (2 * blk, hd), 0) == 0
    outs = []
    for k in range(N_KV_HEADS):
        sl = slice(k * hd, (k + 1) * hd)
        keys = jnp.concatenate([kp_ref[:, sl], kc_ref[:, sl]], axis=0)
        vals = jnp.concatenate([vp_ref[:, sl], vc_ref[:, sl]], axis=0)
        keys = jnp.where(first_row, jnp.zeros_like(keys), keys)
        vals = jnp.where(first_row, jnp.zeros_like(vals), vals)
        q = jnp.concatenate([q_ref[:, (k * g + i) * hd:(k * g + i + 1) * hd] for i in range(g)],
                            axis=0) * scale
        s = lax.dot_general(q, keys, (((1,), (1,)), ((), ())), preferred_element_type=F32)
        s = s + add_ref[0, k * g:(k + 1) * g].reshape(g * blk, 2 * blk)
        m = jnp.max(s, axis=-1, keepdims=True)
        p = jnp.exp(s - m)
        den = jnp.sum(p, axis=-1, keepdims=True)
        o = jnp.dot(p.astype(BF16), vals, preferred_element_type=F32) / den
        outs.extend(o[i * blk:(i + 1) * blk] for i in range(g))
    o_ref[...] = jnp.concatenate(outs, axis=-1).astype(o_ref.dtype)


def _swa(z, add, batch, seq_len):
    blk = ATT_BLOCK
    nblk = seq_len // blk
    M = z.shape[0]
    dq = N_Q_HEADS * HEAD_DIM
    dkv = N_KV_HEADS * HEAD_DIM
    kcol, vcol = dq // dkv, dq // dkv + 1
    cur = lambda c: pl.BlockSpec((blk, dkv), lambda b, n: (b * nblk + n, c))
    prev = lambda c: pl.BlockSpec((blk, dkv), lambda b, n: (b * nblk + jnp.maximum(n - 1, 0), c))
    return pl.pallas_call(
        _swa_kernel,
        grid=(batch, nblk),
        in_specs=[pl.BlockSpec((blk, dq), lambda b, n: (b * nblk + n, 0)),
                  prev(kcol), cur(kcol), prev(vcol), cur(vcol),
                  pl.BlockSpec((1, N_Q_HEADS, blk, 2 * blk), lambda b, n: (jnp.minimum(n, 1), 0, 0, 0))],
        out_specs=pl.BlockSpec((blk, dq), lambda b, n: (b * nblk + n, 0)),
        out_shape=jax.ShapeDtypeStruct((M, dq), BF16),
        compiler_params=_params("arbitrary", "arbitrary"),
        name="swa",
    )(z, z, z, z, z, add)


def _xattn_kernel(rc, hn_ref, h_ref, kv_ref, wq_ref, wo_ref, g_ref, hout_ref, hnout_ref):
    d = hn_ref.shape[1]
    hd = d // X_HEADS
    g = g_ref[...]
    for c in range(hn_ref.shape[0] // rc):
        rows = slice(c * rc, (c + 1) * rc)
        q = jnp.dot(hn_ref[rows, :], wq_ref[...], preferred_element_type=F32).astype(BF16)
        outs = []
        for h in range(X_HEADS):
            k = kv_ref[:, h * hd:(h + 1) * hd]
            v = kv_ref[:, d + h * hd:d + (h + 1) * hd]
            s = lax.dot_general(q[:, h * hd:(h + 1) * hd], k, (((1,), (1,)), ((), ())),
                                preferred_element_type=F32) * (hd ** -0.5)
            m = jnp.max(s, axis=-1, keepdims=True)
            p = jnp.exp(s - m)
            den = jnp.sum(p, axis=-1, keepdims=True)
            outs.append((jnp.dot(p.astype(BF16), v, preferred_element_type=F32) / den).astype(BF16))
        o = jnp.concatenate(outs, axis=-1)
        acc = h_ref[rows, :] + jnp.dot(o, wo_ref[...], preferred_element_type=F32)
        hout_ref[rows, :] = acc
        hnout_ref[rows, :] = _rms(acc, g).astype(hnout_ref.dtype)


def _xattn_block(hn, h, kv, w_q, w_o, layer, g, batch, seq_len, n_mem, *, tq=512):
    M, d = hn.shape
    tq = min(tq, seq_len)
    rc = min(ROW_CHUNK, tq // 2)
    nq = seq_len // tq
    row = pl.BlockSpec((tq, d), lambda b, i: (b * nq + i, 0))
    wspec = pl.BlockSpec((None, d, d), lambda b, i: (layer, 0, 0), pipeline_mode=pl.Buffered(1))
    return pl.pallas_call(
        functools.partial(_xattn_kernel, rc),
        grid=(batch, nq),
        in_specs=[row, row,
                  pl.BlockSpec((n_mem, 2 * d), lambda b, i: (b, 0)),
                  wspec, wspec,
                  pl.BlockSpec((1, d), lambda b, i: (0, 0))],
        out_specs=[row, row],
        out_shape=[jax.ShapeDtypeStruct((M, d), F32), jax.ShapeDtypeStruct((M, d), BF16)],
        compiler_params=_params("arbitrary", "arbitrary"),
        name="xattn",
    )(hn, h, kv, w_q, w_o, g.reshape(1, d))


def kernel(x, mem, norm_mix, norm_xattn, norm_ffn, norm_final, norm_mem, rel_bias, ev_w_in, ev_conv_w, s5_a_re, s5_a_im, s5_log_dt, s5_b_re, s5_b_im, s5_c_re, s5_c_im, s5_d, s5_glu_w, ev_w_out, od_w_qkv, od_b_qkv, od_sinks, od_w_out, xa_w_q, xa_w_kv, xa_w_o, ff_w_gate, ff_w_up, ff_conv_w, ff_conv_b, ff_w_down):
    batch, seq_len, d = x.shape
    n_mem = mem.shape[1]
    depth = norm_mix.shape[0]
    M = batch * seq_len
    a_width = ev_conv_w.shape[-1]
    ev_w_out, od_w_out, xa_w_q, xa_w_o = (w.astype(BF16) for w in (ev_w_out, od_w_out, xa_w_q, xa_w_o))

    s5_tables = jax.vmap(_s5_tables)(s5_a_re, s5_a_im, s5_log_dt, s5_b_re, s5_b_im, s5_c_re, s5_c_im, s5_glu_w)
    s5_selector = _s5_selector(s5_a_re.shape[-1])

    mem_n = _rmsnorm(mem.reshape(batch * n_mem, d), norm_mem)
    h = x.reshape(M, d)
    hn = _rmsnorm(h, norm_mix[0])
    for l in range(depth):
        i = l // 2
        if l % 2 == 0:
            ya = _mixer_a(hn, ev_w_in, i, ev_conv_w, seq_len)
            u = _mm(hn, ev_w_in, i, col0=3 * a_width, out_dtype=F32, name="s5_in")
            ys = _s5(u, s5_tables, s5_selector, s5_d, i, batch, seq_len)
            h, hn = _mm_res_norm([ya, ys], ev_w_out, i, h, norm_xattn[l], name="ev_out")
        else:
            z = _mm(hn, od_w_qkv, i, bias=od_b_qkv, tn=1280, name="qkv")
            o = _swa(z, _swa_table(rel_bias, od_sinks[i]), batch, seq_len)
            h, hn = _mm_res_norm([o], od_w_out, i, h, norm_xattn[l], name="od_out")
        kv = _mm(mem_n, xa_w_kv, l, name="xa_kv")
        h, hn = _xattn_block(hn, h, kv, xa_w_q, xa_w_o, l, norm_ffn[l], batch, seq_len, n_mem)
        act, w_down = _ffn_up(hn, ff_w_gate, ff_w_up, ff_w_down, l, ff_conv_w, ff_conv_b, seq_len)
        if l + 1 < depth:
            h, hn = _mm_res_norm([act], w_down, 0, h, norm_mix[l + 1], tm=256, name="ff_down")
        else:
            _, out = _mm_res_norm([act], w_down, 0, h, norm_final, emit_h=False, norm_dtype=F32,
                                  tm=256, name="ff_down_final")
    return out.reshape(batch, seq_len, d)
```

```python
import functools
import math

import numpy as np
import jax
import jax.numpy as jnp
from jax import lax
from jax.experimental import pallas as pl
from jax.experimental.pallas import tpu as pltpu

F32 = jnp.float32
BF16 = jnp.bfloat16

RMS_EPS = 1e-5
NEG_INF = -1e30
LOG2E = math.log2(math.e)

HEAD_DIM = 64
Q_PER_KV = 8
N_KV_HEADS = 4
N_Q_HEADS = Q_PER_KV * N_KV_HEADS
ATT_BLOCK = 128
N_BUCKETS = 32
MAX_DISTANCE = 128
X_HEADS = 4
S5_GROUP = 16
S5_STATE = 64

LANES = 128
SUBLANES = 8
MXU_DIM = 256
VMEM_LIMIT_BYTES = 60000 * 1024

S5_CHUNK = 16
GROUPS_PER_TILE = LANES // S5_GROUP

ROW_CHUNK = 256


def _params(*sem):
    return pltpu.CompilerParams(dimension_semantics=sem, vmem_limit_bytes=VMEM_LIMIT_BYTES)


def _rms(x, g):
    ms = jnp.mean(x * x, axis=-1, keepdims=True)
    return x * lax.rsqrt(ms + RMS_EPS) * g


def _rmsnorm_kernel(x_ref, g_ref, o_ref):
    o_ref[...] = _rms(x_ref[...], g_ref[...]).astype(o_ref.dtype)


def _rmsnorm(x, g, *, tm=512):
    M, D = x.shape
    tm = min(tm, M)
    return pl.pallas_call(
        _rmsnorm_kernel,
        grid=(M // tm,),
        in_specs=[pl.BlockSpec((tm, D), lambda i: (i, 0)), pl.BlockSpec((1, D), lambda i: (0, 0))],
        out_specs=pl.BlockSpec((tm, D), lambda i: (i, 0)),
        out_shape=jax.ShapeDtypeStruct((M, D), BF16),
        compiler_params=_params("arbitrary"),
        name="rmsnorm",
    )(x, g.reshape(1, D))


def _mm_kernel(has_bias, rc, a_ref, w_ref, *rest):
    if has_bias:
        b_ref, scale_ref, o_ref, wbf_ref = rest
    else:
        o_ref, wbf_ref = rest

    @pl.when(pl.program_id(1) == 0)
    def _():
        wbf_ref[...] = w_ref[...].astype(BF16)

    for c in range(a_ref.shape[0] // rc):
        rows = slice(c * rc, (c + 1) * rc)
        acc = jnp.dot(a_ref[rows, :], wbf_ref[...], preferred_element_type=F32)
        if has_bias:
            acc = (acc + b_ref[...]) * scale_ref[...]
        o_ref[rows, :] = acc.astype(o_ref.dtype)


def _mm(a, w, layer, *, col0=0, n=None, bias=None, col_scale=None, out_dtype=BF16, tm=2048, tn=1024,
        name="mm"):
    M, K = a.shape
    n = w.shape[2] - col0 if n is None else n
    tm, tn = min(tm, M), min(tn, n)
    rc = min(2 * ROW_CHUNK, tm)
    assert M % tm == 0 and n % tn == 0 and col0 % tn == 0 and tm % rc == 0
    cb = col0 // tn
    in_specs = [pl.BlockSpec((tm, K), lambda j, i: (i, 0)),
                pl.BlockSpec((None, K, tn), lambda j, i: (layer, 0, cb + j))]
    args = [a, w]
    if bias is not None:
        in_specs.append(pl.BlockSpec((None, 1, tn), lambda j, i: (layer, 0, cb + j)))
        in_specs.append(pl.BlockSpec((1, tn), lambda j, i: (0, cb + j)))
        args += [bias.reshape(bias.shape[0], 1, -1), col_scale.reshape(1, -1)]
    return pl.pallas_call(
        functools.partial(_mm_kernel, bias is not None, rc),
        grid=(n // tn, M // tm),
        in_specs=in_specs,
        out_specs=pl.BlockSpec((tm, tn), lambda j, i: (i, j)),
        out_shape=jax.ShapeDtypeStruct((M, n), out_dtype),
        scratch_shapes=[pltpu.VMEM((K, tn), BF16)],
        compiler_params=_params("arbitrary", "arbitrary"),
        name=name,
    )(*args)


def _mm_res_norm_kernel(emit_h, rc, n_a, *refs):
    a_refs = refs[:n_a]
    w_ref, h_ref, g_ref = refs[n_a:n_a + 3]
    outs = refs[n_a + 3:]
    hn_ref = outs[-1]
    g = g_ref[...]
    for c in range(h_ref.shape[0] // rc):
        rows = slice(c * rc, (c + 1) * rc)
        acc = h_ref[rows, :]
        k0 = 0
        for a_ref in a_refs:
            k1 = k0 + a_ref.shape[1]
            acc = acc + jnp.dot(a_ref[rows, :], w_ref[k0:k1, :], preferred_element_type=F32)
            k0 = k1
        if emit_h:
            outs[0][rows, :] = acc
        hn_ref[rows, :] = _rms(acc, g).astype(hn_ref.dtype)


def _mm_res_norm(a_parts, w, layer, h, g, *, emit_h=True, norm_dtype=BF16, tm=512, name="mm_res_norm"):
    M, D = h.shape
    assert sum(a.shape[1] for a in a_parts) == w.shape[1]
    tm = min(tm, M)
    rc = min(ROW_CHUNK, tm // 2)
    assert M % tm == 0 and tm % rc == 0
    row = pl.BlockSpec((tm, D), lambda i: (i, 0))
    out_shape = [jax.ShapeDtypeStruct((M, D), norm_dtype)]
    out_specs = [row]
    if emit_h:
        out_shape.insert(0, jax.ShapeDtypeStruct((M, D), F32))
        out_specs.insert(0, row)
    res = pl.pallas_call(
        functools.partial(_mm_res_norm_kernel, emit_h, rc, len(a_parts)),
        grid=(M // tm,),
        in_specs=[pl.BlockSpec((tm, a.shape[1]), lambda i: (i, 0)) for a in a_parts] + [
            pl.BlockSpec((None,) + w.shape[1:], lambda i: (layer, 0, 0), pipeline_mode=pl.Buffered(1)),
            row,
            pl.BlockSpec((1, D), lambda i: (0, 0))],
        out_specs=out_specs,
        out_shape=out_shape,
        compiler_params=_params("arbitrary"),
        name=name,
    )(*a_parts, w, h, g.reshape(1, D))
    return res if emit_h else (None, res[0])


def _conv3_rows(p, prev8, w_ref):
    w0, w1, w2 = w_ref[0:1, :], w_ref[1:2, :], w_ref[2:3, :]
    full = w2 * p + w1 * pltpu.roll(p, 1, 0) + w0 * pltpu.roll(p, 2, 0)
    e = jnp.concatenate([prev8, p[0:SUBLANES]], axis=0)
    f = w2 * e + w1 * pltpu.roll(e, 1, 0) + w0 * pltpu.roll(e, 2, 0)
    return full, f[SUBLANES:2 * SUBLANES]


def _load_halo(halo_ref, tiles_per_seq):
    @pl.when(pl.program_id(1) % tiles_per_seq == 0)
    def _():
        halo_ref[...] = jnp.zeros_like(halo_ref)
    return halo_ref[...]


def _mixer_a_kernel(tiles_per_seq, rc, a_ref, wb_ref, wc_ref, wx_ref, cw_ref, o_ref,
                    wb_s, wc_s, wx_s, halo_s):
    @pl.when(pl.program_id(1) == 0)
    def _():
        wb_s[...] = wb_ref[...].astype(BF16)
        wc_s[...] = wc_ref[...].astype(BF16)
        wx_s[...] = wx_ref[...].astype(BF16)

    prev8 = _load_halo(halo_s, tiles_per_seq)
    for c in range(a_ref.shape[0] // rc):
        r0 = c * rc
        a = a_ref[r0:r0 + rc, :]
        gb = jnp.dot(a, wb_s[...], preferred_element_type=F32)
        p = (jnp.dot(a, wc_s[...], preferred_element_type=F32)
             * jnp.dot(a, wx_s[...], preferred_element_type=F32))
        full, first8 = _conv3_rows(p, prev8, cw_ref)
        prev8 = p[rc - SUBLANES:rc]
        o_ref[r0:r0 + rc, :] = (gb * full).astype(o_ref.dtype)
        o_ref[r0:r0 + SUBLANES, :] = (gb[0:SUBLANES] * first8).astype(o_ref.dtype)
    halo_s[...] = prev8


def _mixer_a(hn, w_in, layer, conv_w, seq_len, *, tm=2048, tn=512):
    M, K = hn.shape
    ca = conv_w.shape[2]
    tm, tn = min(tm, seq_len), min(tn, ca)
    rc = min(ROW_CHUNK, tm)
    assert seq_len % tm == 0 and ca % tn == 0 and tm % rc == 0
    nj = ca // tn
    wspec = lambda off: pl.BlockSpec((None, K, tn), lambda j, i: (layer, 0, off * nj + j))
    return pl.pallas_call(
        functools.partial(_mixer_a_kernel, seq_len // tm, rc),
        grid=(nj, M // tm),
        in_specs=[pl.BlockSpec((tm, K), lambda j, i: (i, 0)),
                  wspec(0), wspec(1), wspec(2),
                  pl.BlockSpec((None, 3, tn), lambda j, i: (layer, 0, j))],
        out_specs=pl.BlockSpec((tm, tn), lambda j, i: (i, j)),
        out_shape=jax.ShapeDtypeStruct((M, ca), BF16),
        scratch_shapes=[pltpu.VMEM((K, tn), BF16)] * 3 + [pltpu.VMEM((SUBLANES, tn), F32)],
        compiler_params=_params("arbitrary", "arbitrary"),
        name="mixer_a",
    )(hn, w_in, w_in, w_in, conv_w)


def _ffn_up_kernel(tiles_per_seq, rc, a_ref, wg_ref, wu_ref, cw_ref, cb_ref, wd_ref, o_ref, wd_bf_ref,
                   wg_s, wu_s, halo_s):
    @pl.when(pl.program_id(1) == 0)
    def _():
        wg_s[...] = wg_ref[...].astype(BF16)
        wu_s[...] = wu_ref[...].astype(BF16)
        wd_bf_ref[...] = wd_ref[...].astype(BF16)

    b = cb_ref[...]
    prev8 = _load_halo(halo_s, tiles_per_seq)
    for c in range(a_ref.shape[0] // rc):
        r0 = c * rc
        a = a_ref[r0:r0 + rc, :]
        gate = jnp.dot(a, wg_s[...], preferred_element_type=F32)
        up = jnp.dot(a, wu_s[...], preferred_element_type=F32)
        full, first8 = _conv3_rows(gate, prev8, cw_ref)
        prev8 = gate[rc - SUBLANES:rc]
        o_ref[r0:r0 + rc, :] = (jax.nn.silu(full + b) * up).astype(o_ref.dtype)
        o_ref[r0:r0 + SUBLANES, :] = (jax.nn.silu(first8 + b) * up[0:SUBLANES]).astype(o_ref.dtype)
    halo_s[...] = prev8


def _ffn_up(hn, w_gate, w_up, w_down, layer, conv_w, conv_b, seq_len, *, tm=2048, tn=512):
    M, K = hn.shape
    dff, d_out = w_down.shape[1:]
    tm, tn = min(tm, seq_len), min(tn, dff)
    rc = min(ROW_CHUNK, tm)
    assert seq_len % tm == 0 and dff % tn == 0 and tm % rc == 0
    wspec = pl.BlockSpec((None, K, tn), lambda j, i: (layer, 0, j))
    return pl.pallas_call(
        functools.partial(_ffn_up_kernel, seq_len // tm, rc),
        grid=(dff // tn, M // tm),
        in_specs=[pl.BlockSpec((tm, K), lambda j, i: (i, 0)),
                  wspec, wspec,
                  pl.BlockSpec((None, 3, tn), lambda j, i: (layer, 0, j)),
                  pl.BlockSpec((None, 1, tn), lambda j, i: (layer, 0, j)),
                  pl.BlockSpec((None, tn, d_out), lambda j, i: (layer, j, 0))],
        out_specs=[pl.BlockSpec((tm, tn), lambda j, i: (i, j)),
                   pl.BlockSpec((None, tn, d_out), lambda j, i: (0, j, 0))],
        out_shape=[jax.ShapeDtypeStruct((M, dff), BF16), jax.ShapeDtypeStruct((1, dff, d_out), BF16)],
        scratch_shapes=[pltpu.VMEM((K, tn), BF16)] * 2 + [pltpu.VMEM((SUBLANES, tn), F32)],
        compiler_params=_params("arbitrary", "arbitrary"),
        name="ffn_up",
    )(hn, w_gate, w_up, conv_w, conv_b.reshape(conv_b.shape[0], 1, dff), w_down)


def _log2(n):
    assert n & (n - 1) == 0
    return n.bit_length() - 1


def _s5_selector(P):
    sel = np.kron(np.eye(2, dtype=np.float32), np.tile(np.eye(P, dtype=np.float32), (1, GROUPS_PER_TILE)))
    return jnp.asarray(sel, BF16)


def _s5_tables(a_re, a_im, log_dt, b_re, b_im, c_re, c_im, glu_w):
    T, gt = S5_CHUNK, GROUPS_PER_TILE
    G, P = a_re.shape
    H = b_re.shape[-1]
    J = G // gt
    dt = jnp.exp(log_dt)[:, None]
    lr, li = a_re * dt, a_im * dt

    def a_pow(taus):
        tau = jnp.asarray(np.asarray(taus, np.float32))[:, None, None]
        mag = jnp.exp(tau * lr)
        return lax.optimization_barrier((mag * jnp.cos(tau * li), mag * jnp.sin(tau * li)))

    pw_re, pw_im = a_pow(np.arange(T + 1))
    nr, ni = pw_re[1] - 1.0, pw_im[1]
    den = a_re * a_re + a_im * a_im
    qr, qi = (nr * a_re + ni * a_im) / den, (ni * a_re - nr * a_im) / den
    bt_re, bt_im = b_re.transpose(0, 2, 1), b_im.transpose(0, 2, 1)
    bb_re = qr[:, None, :] * bt_re - qi[:, None, :] * bt_im
    bb_im = qr[:, None, :] * bt_im + qi[:, None, :] * bt_re
    cp_re = c_re[None] * pw_re[:, :, None, :] - c_im[None] * pw_im[:, :, None, :]
    cp_im = c_re[None] * pw_im[:, :, None, :] + c_im[None] * pw_re[:, :, None, :]
    rev_re, rev_im = a_pow(T - 1 - np.arange(T))
    wv_re = rev_re[:, :, None, :] * bb_re[None] - rev_im[:, :, None, :] * bb_im[None]
    wv_im = rev_re[:, :, None, :] * bb_im[None] + rev_im[:, :, None, :] * bb_re[None]

    def tiles(compact):
        full = jnp.dot(compact.astype(BF16), jnp.asarray(np.tile(np.eye(H, dtype=np.float32), (1, gt)), BF16),
                       preferred_element_type=F32)
        rg = lax.broadcasted_iota(jnp.int32, (LANES, LANES), 0) // H
        cg = lax.broadcasted_iota(jnp.int32, (LANES, LANES), 1) // H
        return jnp.where(rg == cg, full, 0.0).astype(BF16)

    glu = tiles(glu_w.reshape(J, gt * H, H))

    wvc = jnp.concatenate([wv_re, wv_im], axis=-1).astype(BF16).reshape(T, J, gt * H, 2 * P)
    cpc = jnp.concatenate([cp_re, -cp_im], axis=-1).astype(BF16).reshape(T + 1, J, gt * H, 2 * P)
    a_chunk = jnp.concatenate([pw_re[T].reshape(J, 1, gt * P), pw_im[T].reshape(J, 1, gt * P)], axis=-1)
    return wvc, cpc, a_chunk, glu


def _gelu_tanh(x):
    c = math.sqrt(2.0 / math.pi)
    return 0.5 * x * (1.0 + jnp.tanh(c * (x + 0.044715 * (x * x * x))))


def _expand_groups(compact_ref, sel_ref, out_ref, steps_per_dot=4):
    nblk, rows, _ = compact_ref.shape
    lh, lp = _log2(S5_GROUP), _log2(out_ref.shape[1] // (2 * GROUPS_PER_TILE))
    for t0 in range(0, nblk, steps_per_dot):
        n = rows * min(steps_per_dot, nblk - t0)
        full = jnp.dot(compact_ref[t0:t0 + n // rows].reshape(n, -1), sel_ref[...],
                       preferred_element_type=F32)
        a = (lax.broadcasted_iota(jnp.int32, full.shape, 0) >> lh) & (GROUPS_PER_TILE - 1)
        b = (lax.broadcasted_iota(jnp.int32, full.shape, 1) >> lp) & (GROUPS_PER_TILE - 1)
        out_ref[t0 * rows:t0 * rows + n, :] = jnp.where(a == b, full, 0.0).astype(BF16)


def _s5_kernel(u_ref, wvc_ref, cpc_ref, sel_ref, ac_ref, d_ref, glu_ref, o_ref,
               mi_s, wv_s, cp_s, ubf_s, v_s, sp_s, y_s, ynat_s):
    T = S5_CHUNK
    nc = u_ref.shape[0] // T
    ns = ac_ref.shape[-1] // 2
    pair = MXU_DIM // LANES

    @pl.when(pl.program_id(1) == 0)
    def _():
        _expand_groups(wvc_ref, sel_ref, wv_s)
        _expand_groups(cpc_ref, sel_ref, cp_s)
        toe = lax.dot_general(wv_s[(T - 1) * LANES:T * LANES, :], cp_s[0:T * LANES, :],
                              (((1,), (1,)), ((), ())), preferred_element_type=F32).astype(BF16)
        for s in range(T):
            for t in range(T):
                if t >= s:
                    tile = toe[:, (t - s) * LANES:(t - s + 1) * LANES]
                elif t // pair == s // pair:
                    tile = jnp.zeros((LANES, LANES), BF16)
                else:
                    continue
                mi_s[s * LANES:(s + 1) * LANES, t * LANES:(t + 1) * LANES] = tile

    for t in range(T):
        ubf_s[:, t * LANES:(t + 1) * LANES] = u_ref[pl.ds(t, nc, stride=T), :].astype(BF16)
    v_s[...] = jnp.dot(ubf_s[...], wv_s[...], preferred_element_type=F32)
    ar, ai = ac_ref[0, :, 0:ns], ac_ref[0, :, ns:2 * ns]

    def step(c, carry):
        sr, si = carry
        sp_s[pl.ds(c, 1), 0:ns] = sr
        sp_s[pl.ds(c, 1), ns:2 * ns] = si
        vr = v_s[pl.ds(c, 1), 0:ns]
        vi = v_s[pl.ds(c, 1), ns:2 * ns]
        return ar * sr - ai * si + vr, ar * si + ai * sr + vi

    zero = jnp.zeros((1, ns), F32)
    lax.fori_loop(0, nc, step, (zero, zero))
    sp = sp_s[...].astype(BF16)
    for cb in range(T // pair):
        kk = (cb + 1) * MXU_DIM
        cols = slice(cb * MXU_DIM, (cb + 1) * MXU_DIM)
        wo_rows = slice(LANES + cb * MXU_DIM, LANES + (cb + 1) * MXU_DIM)
        y_s[:, cols] = (jnp.dot(ubf_s[:, 0:kk], mi_s[0:kk, cols], preferred_element_type=F32)
                        + lax.dot_general(sp, cp_s[wo_rows, :], (((1,), (1,)), ((), ())),
                                          preferred_element_type=F32))
    for t in range(T):
        ynat_s[pl.ds(t, nc, stride=T), :] = y_s[:, t * LANES:(t + 1) * LANES]
    y = ynat_s[...] + d_ref[...] * u_ref[...]
    yg = _gelu_tanh(y)
    gate = jnp.dot(yg.astype(BF16), glu_ref[0], preferred_element_type=F32)
    o_ref[...] = (yg * jax.nn.sigmoid(gate)).astype(o_ref.dtype)


def _s5(u, tables, sel, d, layer, batch, seq_len):
    wvc, cpc, a_chunk, glu = tables
    J = glu.shape[1]
    M, width = u.shape
    T = S5_CHUNK
    nc = seq_len // T
    n2 = a_chunk.shape[-1]
    per_tile = lambda x: pl.BlockSpec((None, 1) + x.shape[2:], lambda j, b: (layer, j) + (0,) * (x.ndim - 2))
    compact = lambda x: pl.BlockSpec((None, x.shape[1], None) + x.shape[3:], lambda j, b: (layer, 0, j, 0, 0))
    return pl.pallas_call(
        _s5_kernel,
        grid=(J, batch),
        in_specs=[pl.BlockSpec((seq_len, LANES), lambda j, b: (b, j)),
                  compact(wvc), compact(cpc),
                  pl.BlockSpec(sel.shape, lambda j, b: (0, 0)),
                  per_tile(a_chunk),
                  pl.BlockSpec((None, 1, LANES), lambda j, b: (layer, 0, j)),
                  per_tile(glu)],
        out_specs=pl.BlockSpec((seq_len, LANES), lambda j, b: (b, j)),
        out_shape=jax.ShapeDtypeStruct((M, width), BF16),
        scratch_shapes=[pltpu.VMEM((T * LANES, T * LANES), BF16),
                        pltpu.VMEM((T * LANES, n2), BF16),
                        pltpu.VMEM(((T + 1) * LANES, n2), BF16),
                        pltpu.VMEM((nc, T * LANES), BF16),
                        pltpu.VMEM((nc, n2), F32),
                        pltpu.VMEM((nc, n2), F32),
                        pltpu.VMEM((nc, T * LANES), F32),
                        pltpu.VMEM((seq_len, LANES), F32)],
        compiler_params=_params("arbitrary", "arbitrary"),
        name="s5",
    )(u, wvc, cpc, sel, a_chunk, d.reshape(d.shape[0], 1, width), glu)


def _t5_bucket_np(rel):
    max_exact = N_BUCKETS // 2
    n = np.maximum(rel, 0)
    nf = np.maximum(n, max_exact).astype(np.float32)
    large = max_exact + (np.log(nf / np.float32(max_exact)) / np.float32(math.log(MAX_DISTANCE / max_exact))
                         * np.float32(N_BUCKETS - max_exact)).astype(np.int32)
    large = np.minimum(large, N_BUCKETS - 1)
    return np.where(n < max_exact, n, large).astype(np.int32)


def _swa_table_kernel(bias_ref, sink_ref, bucket_ref, valid_ref, add_ref):
    bucket = bucket_ref[...]
    valid = valid_ref[0] != 0
    sink_row = lax.broadcasted_iota(jnp.int32, bucket.shape, 0) == 0

    @pl.loop(0, add_ref.shape[1])
    def _(h):
        acc = jnp.full(bucket.shape, NEG_INF, F32)
        for b in range(N_BUCKETS):
            acc = jnp.where(bucket == b, bias_ref[b, h] * LOG2E, acc)
        acc = jnp.where(valid, acc, NEG_INF)
        add_ref[0, h] = jnp.where(sink_row, sink_ref[h] * LOG2E, acc)


def _swa_table(rel_bias, sinks):
    blk = ATT_BLOCK
    qi = np.arange(blk)[:, None]
    kj = np.arange(2 * blk)[None, :]
    rel = qi + blk - kj
    in_window = (rel >= 0) & (rel < blk)
    valid = np.stack([(in_window & (kj >= blk)).T, in_window.T]).astype(np.int32)
    bucket = np.ascontiguousarray(_t5_bucket_np(rel).T)
    return pl.pallas_call(
        _swa_table_kernel,
        grid=(2,),
        in_specs=[pl.BlockSpec(memory_space=pltpu.SMEM),
                  pl.BlockSpec(memory_space=pltpu.SMEM),
                  pl.BlockSpec((2 * blk, blk), lambda v: (0, 0)),
                  pl.BlockSpec((1, 2 * blk, blk), lambda v: (v, 0, 0))],
        out_specs=pl.BlockSpec((1, N_Q_HEADS, 2 * blk, blk), lambda v: (v, 0, 0, 0)),
        out_shape=jax.ShapeDtypeStruct((2, N_Q_HEADS, 2 * blk, blk), F32),
        compiler_params=_params("arbitrary"),
        name="swa_table",
    )(rel_bias, sinks, jnp.asarray(bucket), jnp.asarray(valid))


def _swa_kernel(q_ref, kp_ref, kc_ref, vp_ref, vc_ref, add_ref, o_ref):
    blk, hd, g = ATT_BLOCK, HEAD_DIM, Q_PER_KV
    first_row = lax.broadcasted_iota(jnp.int32, (2 * blk, hd), 0) == 0
    outs = []
    for k in range(N_KV_HEADS):
        sl = slice(k * hd, (k + 1) * hd)
        keys = jnp.concatenate([kp_ref[:, sl], kc_ref[:, sl]], axis=0)
        vals = jnp.concatenate([vp_ref[:, sl], vc_ref[:, sl]], axis=0)
        keys = jnp.where(first_row, jnp.zeros_like(keys), keys)
        vals = jnp.where(first_row, jnp.zeros_like(vals), vals)
        q = jnp.concatenate([q_ref[:, (k * g + i) * hd:(k * g + i + 1) * hd] for i in range(g)],
                            axis=0)
        s = lax.dot_general(keys, q, (((1,), (1,)), ((), ())), preferred_element_type=F32)
        ps, dens = [], []
        for i in range(g):
            si = s[:, i * blk:(i + 1) * blk] + add_ref[0, k * g + i]
            pi = jnp.exp2(si - jnp.max(si, axis=0, keepdims=True))
            dens.append(jnp.sum(pi, axis=0, keepdims=True))
            ps.append(pi.astype(BF16))
        o = lax.dot_general(vals, jnp.concatenate(ps, axis=1), (((0,), (0,)), ((), ())),
                            preferred_element_type=F32) / jnp.concatenate(dens, axis=1)
        outs.extend(o[:, i * blk:(i + 1) * blk].T for i in range(g))
    o_ref[...] = jnp.concatenate(outs, axis=-1).astype(o_ref.dtype)


def _swa(z, add, batch, seq_len):
    blk = ATT_BLOCK
    nblk = seq_len // blk
    M = z.shape[0]
    dq = N_Q_HEADS * HEAD_DIM
    dkv = N_KV_HEADS * HEAD_DIM
    kcol, vcol = dq // dkv, dq // dkv + 1
    cur = lambda c: pl.BlockSpec((blk, dkv), lambda b, n: (b * nblk + n, c))
    prev = lambda c: pl.BlockSpec((blk, dkv), lambda b, n: (b * nblk + jnp.maximum(n - 1, 0), c))
    return pl.pallas_call(
        _swa_kernel,
        grid=(batch, nblk),
        in_specs=[pl.BlockSpec((blk, dq), lambda b, n: (b * nblk + n, 0)),
                  prev(kcol), cur(kcol), prev(vcol), cur(vcol),
                  pl.BlockSpec((1, N_Q_HEADS, 2 * blk, blk), lambda b, n: (jnp.minimum(n, 1), 0, 0, 0))],
        out_specs=pl.BlockSpec((blk, dq), lambda b, n: (b * nblk + n, 0)),
        out_shape=jax.ShapeDtypeStruct((M, dq), BF16),
        compiler_params=_params("arbitrary", "arbitrary"),
        name="swa",
    )(z, z, z, z, z, add)


def _xattn_kernel(rc, hn_ref, h_ref, kv_ref, wq_ref, wo_ref, g_ref, hout_ref, hnout_ref):
    d = hn_ref.shape[1]
    hd = d // X_HEADS
    g = g_ref[...]
    for c in range(hn_ref.shape[0] // rc):
        rows = slice(c * rc, (c + 1) * rc)
        q = jnp.dot(hn_ref[rows, :], wq_ref[...], preferred_element_type=F32).astype(BF16)
        outs = []
        for h in range(X_HEADS):
            k = kv_ref[:, h * hd:(h + 1) * hd]
            v = kv_ref[:, d + h * hd:d + (h + 1) * hd]
            s = lax.dot_general(q[:, h * hd:(h + 1) * hd], k, (((1,), (1,)), ((), ())),
                                preferred_element_type=F32) * (hd ** -0.5)
            m = jnp.max(s, axis=-1, keepdims=True)
            p = jnp.exp(s - m)
            den = jnp.sum(p, axis=-1, keepdims=True)
            outs.append((jnp.dot(p.astype(BF16), v, preferred_element_type=F32) / den).astype(BF16))
        o = jnp.concatenate(outs, axis=-1)
        acc = h_ref[rows, :] + jnp.dot(o, wo_ref[...], preferred_element_type=F32)
        hout_ref[rows, :] = acc
        hnout_ref[rows, :] = _rms(acc, g).astype(hnout_ref.dtype)


def _xattn_block(hn, h, kv, w_q, w_o, layer, g, batch, seq_len, n_mem, *, tq=512):
    M, d = hn.shape
    tq = min(tq, seq_len)
    rc = min(ROW_CHUNK, tq // 2)
    nq = seq_len // tq
    row = pl.BlockSpec((tq, d), lambda b, i: (b * nq + i, 0))
    wspec = pl.BlockSpec((None, d, d), lambda b, i: (layer, 0, 0), pipeline_mode=pl.Buffered(1))
    return pl.pallas_call(
        functools.partial(_xattn_kernel, rc),
        grid=(batch, nq),
        in_specs=[row, row,
                  pl.BlockSpec((n_mem, 2 * d), lambda b, i: (b, 0)),
                  wspec, wspec,
                  pl.BlockSpec((1, d), lambda b, i: (0, 0))],
        out_specs=[row, row],
        out_shape=[jax.ShapeDtypeStruct((M, d), F32), jax.ShapeDtypeStruct((M, d), BF16)],
        compiler_params=_params("arbitrary", "arbitrary"),
        name="xattn",
    )(hn, h, kv, w_q, w_o, g.reshape(1, d))


def kernel(x, mem, norm_mix, norm_xattn, norm_ffn, norm_final, norm_mem, rel_bias, ev_w_in, ev_conv_w, s5_a_re, s5_a_im, s5_log_dt, s5_b_re, s5_b_im, s5_c_re, s5_c_im, s5_d, s5_glu_w, ev_w_out, od_w_qkv, od_b_qkv, od_sinks, od_w_out, xa_w_q, xa_w_kv, xa_w_o, ff_w_gate, ff_w_up, ff_conv_w, ff_conv_b, ff_w_down):
    batch, seq_len, d = x.shape
    n_mem = mem.shape[1]
    depth = norm_mix.shape[0]
    M = batch * seq_len
    a_width = ev_conv_w.shape[-1]
    ev_w_out, od_w_out, xa_w_q, xa_w_o = (w.astype(BF16) for w in (ev_w_out, od_w_out, xa_w_q, xa_w_o))

    s5_tables = jax.vmap(_s5_tables)(s5_a_re, s5_a_im, s5_log_dt, s5_b_re, s5_b_im, s5_c_re, s5_c_im, s5_glu_w)
    s5_selector = _s5_selector(s5_a_re.shape[-1])
    n_q = N_Q_HEADS * HEAD_DIM
    qkv_scale = jnp.asarray(np.where(np.arange(od_w_qkv.shape[-1]) < n_q, HEAD_DIM ** -0.5 * LOG2E, 1.0), F32)

    mem_n = _rmsnorm(mem.reshape(batch * n_mem, d), norm_mem)
    h = x.reshape(M, d)
    hn = _rmsnorm(h, norm_mix[0])
    for l in range(depth):
        i = l // 2
        if l % 2 == 0:
            ya = _mixer_a(hn, ev_w_in, i, ev_conv_w, seq_len)
            u = _mm(hn, ev_w_in, i, col0=3 * a_width, out_dtype=F32, name="s5_in")
            ys = _s5(u, s5_tables, s5_selector, s5_d, i, batch, seq_len)
            h, hn = _mm_res_norm([ya, ys], ev_w_out, i, h, norm_xattn[l], name="ev_out")
        else:
            z = _mm(hn, od_w_qkv, i, bias=od_b_qkv, col_scale=qkv_scale, tn=1280, name="qkv")
            o = _swa(z, _swa_table(rel_bias, od_sinks[i]), batch, seq_len)
            h, hn = _mm_res_norm([o], od_w_out, i, h, norm_xattn[l], name="od_out")
        kv = _mm(mem_n, xa_w_kv, l, name="xa_kv")
        h, hn = _xattn_block(hn, h, kv, xa_w_q, xa_w_o, l, norm_ffn[l], batch, seq_len, n_mem)
        act, w_down = _ffn_up(hn, ff_w_gate, ff_w_up, ff_w_down, l, ff_conv_w, ff_conv_b, seq_len)
        if l + 1 < depth:
            h, hn = _mm_res_norm([act], w_down, 0, h, norm_mix[l + 1], tm=256, name="ff_down")
        else:
            _, out = _mm_res_norm([act], w_down, 0, h, norm_final, emit_h=False, norm_dtype=F32,
                                  tm=256, name="ff_down_final")
    return out.reshape(batch, seq_len, d)
```

```python
import functools
import math

import numpy as np
import jax
import jax.numpy as jnp
from jax import lax
from jax.experimental import pallas as pl
from jax.experimental.pallas import tpu as pltpu

F32 = jnp.float32
BF16 = jnp.bfloat16

RMS_EPS = 1e-5
NEG_INF = -1e30

HEAD_DIM = 64
Q_PER_KV = 8
N_KV_HEADS = 4
N_Q_HEADS = Q_PER_KV * N_KV_HEADS
ATT_BLOCK = 128
N_BUCKETS = 32
MAX_DISTANCE = 128
X_HEADS = 4
S5_GROUP = 16

LANES = 128
SUBLANES = 8
MXU_DIM = 256
VMEM_LIMIT_BYTES = 60000 * 1024

S5_CHUNK = 16
GROUPS_PER_TILE = LANES // S5_GROUP

ROW_CHUNK = 256


def _params(*sem):
    return pltpu.CompilerParams(dimension_semantics=sem, vmem_limit_bytes=VMEM_LIMIT_BYTES)


def _rms(x, g):
    ms = jnp.mean(x * x, axis=-1, keepdims=True)
    return x * lax.rsqrt(ms + RMS_EPS) * g


def _rmsnorm_kernel(x_ref, g_ref, o_ref):
    o_ref[...] = _rms(x_ref[...], g_ref[...]).astype(o_ref.dtype)


def _rmsnorm(x, g, *, tm=512):
    M, D = x.shape
    tm = min(tm, M)
    return pl.pallas_call(
        _rmsnorm_kernel,
        grid=(M // tm,),
        in_specs=[pl.BlockSpec((tm, D), lambda i: (i, 0)), pl.BlockSpec((1, D), lambda i: (0, 0))],
        out_specs=pl.BlockSpec((tm, D), lambda i: (i, 0)),
        out_shape=jax.ShapeDtypeStruct((M, D), BF16),
        compiler_params=_params("arbitrary"),
        name="rmsnorm",
    )(x, g.reshape(1, D))


def _mm_kernel(has_bias, rc, a_ref, w_ref, *rest):
    if has_bias:
        b_ref, o_ref, wbf_ref = rest
    else:
        o_ref, wbf_ref = rest

    @pl.when(pl.program_id(1) == 0)
    def _():
        wbf_ref[...] = w_ref[...].astype(BF16)

    for c in range(a_ref.shape[0] // rc):
        rows = slice(c * rc, (c + 1) * rc)
        acc = jnp.dot(a_ref[rows, :], wbf_ref[...], preferred_element_type=F32)
        if has_bias:
            acc = acc + b_ref[...]
        o_ref[rows, :] = acc.astype(o_ref.dtype)


def _mm(a, w, layer, *, col0=0, n=None, bias=None, out_dtype=BF16, tm=2048, tn=1024, name="mm"):
    M, K = a.shape
    n = w.shape[2] - col0 if n is None else n
    tm, tn = min(tm, M), min(tn, n)
    rc = min(2 * ROW_CHUNK, tm)
    assert M % tm == 0 and n % tn == 0 and col0 % tn == 0 and tm % rc == 0
    cb = col0 // tn
    in_specs = [pl.BlockSpec((tm, K), lambda j, i: (i, 0)),
                pl.BlockSpec((None, K, tn), lambda j, i: (layer, 0, cb + j))]
    args = [a, w]
    if bias is not None:
        in_specs.append(pl.BlockSpec((None, 1, tn), lambda j, i: (layer, 0, cb + j)))
        args.append(bias.reshape(bias.shape[0], 1, -1))
    return pl.pallas_call(
        functools.partial(_mm_kernel, bias is not None, rc),
        grid=(n // tn, M // tm),
        in_specs=in_specs,
        out_specs=pl.BlockSpec((tm, tn), lambda j, i: (i, j)),
        out_shape=jax.ShapeDtypeStruct((M, n), out_dtype),
        scratch_shapes=[pltpu.VMEM((K, tn), BF16)],
        compiler_params=_params("arbitrary", "arbitrary"),
        name=name,
    )(*args)


def _mm_res_norm_kernel(emit_h, rc, n_a, *refs):
    a_refs = refs[:n_a]
    w_ref, h_ref, g_ref = refs[n_a:n_a + 3]
    outs = refs[n_a + 3:]
    hn_ref = outs[-1]
    g = g_ref[...]
    for c in range(h_ref.shape[0] // rc):
        rows = slice(c * rc, (c + 1) * rc)
        acc = h_ref[rows, :]
        k0 = 0
        for a_ref in a_refs:
            k1 = k0 + a_ref.shape[1]
            acc = acc + jnp.dot(a_ref[rows, :], w_ref[k0:k1, :], preferred_element_type=F32)
            k0 = k1
        if emit_h:
            outs[0][rows, :] = acc
        hn_ref[rows, :] = _rms(acc, g).astype(hn_ref.dtype)


def _mm_res_norm(a_parts, w, layer, h, g, *, emit_h=True, norm_dtype=BF16, tm=512, name="mm_res_norm"):
    M, D = h.shape
    assert sum(a.shape[1] for a in a_parts) == w.shape[1]
    tm = min(tm, M)
    rc = min(ROW_CHUNK, tm // 2)
    assert M % tm == 0 and tm % rc == 0
    row = pl.BlockSpec((tm, D), lambda i: (i, 0))
    out_shape = [jax.ShapeDtypeStruct((M, D), norm_dtype)]
    out_specs = [row]
    if emit_h:
        out_shape.insert(0, jax.ShapeDtypeStruct((M, D), F32))
        out_specs.insert(0, row)
    res = pl.pallas_call(
        functools.partial(_mm_res_norm_kernel, emit_h, rc, len(a_parts)),
        grid=(M // tm,),
        in_specs=[pl.BlockSpec((tm, a.shape[1]), lambda i: (i, 0)) for a in a_parts] + [
            pl.BlockSpec((None,) + w.shape[1:], lambda i: (layer, 0, 0), pipeline_mode=pl.Buffered(1)),
            row,
            pl.BlockSpec((1, D), lambda i: (0, 0))],
        out_specs=out_specs,
        out_shape=out_shape,
        compiler_params=_params("arbitrary"),
        name=name,
    )(*a_parts, w, h, g.reshape(1, D))
    return res if emit_h else (None, res[0])


def _conv3_rows(p, prev8, w_ref):
    w0, w1, w2 = w_ref[0:1, :], w_ref[1:2, :], w_ref[2:3, :]
    full = w2 * p + w1 * pltpu.roll(p, 1, 0) + w0 * pltpu.roll(p, 2, 0)
    e = jnp.concatenate([prev8, p[0:SUBLANES]], axis=0)
    f = w2 * e + w1 * pltpu.roll(e, 1, 0) + w0 * pltpu.roll(e, 2, 0)
    return full, f[SUBLANES:2 * SUBLANES]


def _load_halo(halo_ref, tiles_per_seq):
    @pl.when(pl.program_id(1) % tiles_per_seq == 0)
    def _():
        halo_ref[...] = jnp.zeros_like(halo_ref)
    return halo_ref[...]


def _mixer_a_kernel(tiles_per_seq, rc, a_ref, wb_ref, wc_ref, wx_ref, cw_ref, o_ref,
                    wb_s, wc_s, wx_s, halo_s):
    @pl.when(pl.program_id(1) == 0)
    def _():
        wb_s[...] = wb_ref[...].astype(BF16)
        wc_s[...] = wc_ref[...].astype(BF16)
        wx_s[...] = wx_ref[...].astype(BF16)

    prev8 = _load_halo(halo_s, tiles_per_seq)
    for c in range(a_ref.shape[0] // rc):
        r0 = c * rc
        a = a_ref[r0:r0 + rc, :]
        gb = jnp.dot(a, wb_s[...], preferred_element_type=F32)
        p = (jnp.dot(a, wc_s[...], preferred_element_type=F32)
             * jnp.dot(a, wx_s[...], preferred_element_type=F32))
        full, first8 = _conv3_rows(p, prev8, cw_ref)
        prev8 = p[rc - SUBLANES:rc]
        o_ref[r0:r0 + rc, :] = (gb * full).astype(o_ref.dtype)
        o_ref[r0:r0 + SUBLANES, :] = (gb[0:SUBLANES] * first8).astype(o_ref.dtype)
    halo_s[...] = prev8


def _mixer_a(hn, w_in, layer, conv_w, seq_len, *, tm=2048, tn=512):
    M, K = hn.shape
    ca = conv_w.shape[2]
    tm, tn = min(tm, seq_len), min(tn, ca)
    rc = min(ROW_CHUNK, tm)
    assert seq_len % tm == 0 and ca % tn == 0 and tm % rc == 0
    nj = ca // tn
    wspec = lambda off: pl.BlockSpec((None, K, tn), lambda j, i: (layer, 0, off * nj + j))
    return pl.pallas_call(
        functools.partial(_mixer_a_kernel, seq_len // tm, rc),
        grid=(nj, M // tm),
        in_specs=[pl.BlockSpec((tm, K), lambda j, i: (i, 0)),
                  wspec(0), wspec(1), wspec(2),
                  pl.BlockSpec((None, 3, tn), lambda j, i: (layer, 0, j))],
        out_specs=pl.BlockSpec((tm, tn), lambda j, i: (i, j)),
        out_shape=jax.ShapeDtypeStruct((M, ca), BF16),
        scratch_shapes=[pltpu.VMEM((K, tn), BF16)] * 3 + [pltpu.VMEM((SUBLANES, tn), F32)],
        compiler_params=_params("arbitrary", "arbitrary"),
        name="mixer_a",
    )(hn, w_in, w_in, w_in, conv_w)


def _ffn_up_kernel(tiles_per_seq, rc, a_ref, wg_ref, wu_ref, cw_ref, cb_ref, wd_ref, o_ref, wd_bf_ref,
                   wg_s, wu_s, halo_s):
    @pl.when(pl.program_id(1) == 0)
    def _():
        wg_s[...] = wg_ref[...].astype(BF16)
        wu_s[...] = wu_ref[...].astype(BF16)
        wd_bf_ref[...] = wd_ref[...].astype(BF16)

    b = cb_ref[...]
    prev8 = _load_halo(halo_s, tiles_per_seq)
    for c in range(a_ref.shape[0] // rc):
        r0 = c * rc
        a = a_ref[r0:r0 + rc, :]
        gate = jnp.dot(a, wg_s[...], preferred_element_type=F32)
        up = jnp.dot(a, wu_s[...], preferred_element_type=F32)
        full, first8 = _conv3_rows(gate, prev8, cw_ref)
        prev8 = gate[rc - SUBLANES:rc]
        o_ref[r0:r0 + rc, :] = (jax.nn.silu(full + b) * up).astype(o_ref.dtype)
        o_ref[r0:r0 + SUBLANES, :] = (jax.nn.silu(first8 + b) * up[0:SUBLANES]).astype(o_ref.dtype)
    halo_s[...] = prev8


def _ffn_up(hn, w_gate, w_up, w_down, layer, conv_w, conv_b, seq_len, *, tm=2048, tn=512):
    M, K = hn.shape
    dff, d_out = w_down.shape[1:]
    tm, tn = min(tm, seq_len), min(tn, dff)
    rc = min(ROW_CHUNK, tm)
    assert seq_len % tm == 0 and dff % tn == 0 and tm % rc == 0
    wspec = pl.BlockSpec((None, K, tn), lambda j, i: (layer, 0, j))
    return pl.pallas_call(
        functools.partial(_ffn_up_kernel, seq_len // tm, rc),
        grid=(dff // tn, M // tm),
        in_specs=[pl.BlockSpec((tm, K), lambda j, i: (i, 0)),
                  wspec, wspec,
                  pl.BlockSpec((None, 3, tn), lambda j, i: (layer, 0, j)),
                  pl.BlockSpec((None, 1, tn), lambda j, i: (layer, 0, j)),
                  pl.BlockSpec((None, tn, d_out), lambda j, i: (layer, j, 0))],
        out_specs=[pl.BlockSpec((tm, tn), lambda j, i: (i, j)),
                   pl.BlockSpec((None, tn, d_out), lambda j, i: (0, j, 0))],
        out_shape=[jax.ShapeDtypeStruct((M, dff), BF16), jax.ShapeDtypeStruct((1, dff, d_out), BF16)],
        scratch_shapes=[pltpu.VMEM((K, tn), BF16)] * 2 + [pltpu.VMEM((SUBLANES, tn), F32)],
        compiler_params=_params("arbitrary", "arbitrary"),
        name="ffn_up",
    )(hn, w_gate, w_up, conv_w, conv_b.reshape(conv_b.shape[0], 1, dff), w_down)


def _log2(n):
    assert n & (n - 1) == 0
    return n.bit_length() - 1


def _s5_selector(P):
    sel = np.kron(np.eye(2, dtype=np.float32), np.tile(np.eye(P, dtype=np.float32), (1, GROUPS_PER_TILE)))
    return jnp.asarray(sel, BF16)


def _s5_tables(a_re, a_im, log_dt, b_re, b_im, c_re, c_im, glu_w):
    T, gt = S5_CHUNK, GROUPS_PER_TILE
    G, P = a_re.shape
    H = b_re.shape[-1]
    J = G // gt
    dt = jnp.exp(log_dt)[:, None]
    lr, li = a_re * dt, a_im * dt

    def a_pow(taus):
        tau = jnp.asarray(np.asarray(taus, np.float32))[:, None, None]
        mag = jnp.exp(tau * lr)
        return mag * jnp.cos(tau * li), mag * jnp.sin(tau * li)

    pw_re, pw_im = a_pow(np.arange(T + 1))
    nr, ni = pw_re[1] - 1.0, pw_im[1]
    den = a_re * a_re + a_im * a_im
    qr, qi = (nr * a_re + ni * a_im) / den, (ni * a_re - nr * a_im) / den
    bt_re, bt_im = b_re.transpose(0, 2, 1), b_im.transpose(0, 2, 1)
    bb_re = qr[:, None, :] * bt_re - qi[:, None, :] * bt_im
    bb_im = qr[:, None, :] * bt_im + qi[:, None, :] * bt_re
    cp_re = c_re[None] * pw_re[:, :, None, :] - c_im[None] * pw_im[:, :, None, :]
    cp_im = c_re[None] * pw_im[:, :, None, :] + c_im[None] * pw_re[:, :, None, :]
    rev_re, rev_im = a_pow(T - 1 - np.arange(T))
    wv_re = rev_re[:, :, None, :] * bb_re[None] - rev_im[:, :, None, :] * bb_im[None]
    wv_im = rev_re[:, :, None, :] * bb_im[None] + rev_im[:, :, None, :] * bb_re[None]

    def tiles(compact):
        full = jnp.dot(compact.astype(BF16), jnp.asarray(np.tile(np.eye(H, dtype=np.float32), (1, gt)), BF16),
                       preferred_element_type=F32)
        rg = lax.broadcasted_iota(jnp.int32, (LANES, LANES), 0) // H
        cg = lax.broadcasted_iota(jnp.int32, (LANES, LANES), 1) // H
        return jnp.where(rg == cg, full, 0.0).astype(BF16)

    glu = tiles(glu_w.reshape(J, gt * H, H))

    wvc = jnp.concatenate([wv_re, wv_im], axis=-1).astype(BF16).reshape(T, J, gt * H, 2 * P)
    cpc = jnp.concatenate([cp_re, -cp_im], axis=-1).astype(BF16).reshape(T + 1, J, gt * H, 2 * P)
    a_chunk = jnp.concatenate([pw_re[T].reshape(J, 1, gt * P), pw_im[T].reshape(J, 1, gt * P)], axis=-1)
    return wvc, cpc, a_chunk, glu


def _gelu_tanh(x):
    c = math.sqrt(2.0 / math.pi)
    return 0.5 * x * (1.0 + jnp.tanh(c * (x + 0.044715 * (x * x * x))))


def _expand_groups(compact_ref, sel_ref, out_ref, steps_per_dot=4):
    nblk, rows, _ = compact_ref.shape
    lh, lp = _log2(S5_GROUP), _log2(out_ref.shape[1] // (2 * GROUPS_PER_TILE))
    for t0 in range(0, nblk, steps_per_dot):
        n = rows * min(steps_per_dot, nblk - t0)
        full = jnp.dot(compact_ref[t0:t0 + n // rows].reshape(n, -1), sel_ref[...],
                       preferred_element_type=F32)
        a = (lax.broadcasted_iota(jnp.int32, full.shape, 0) >> lh) & (GROUPS_PER_TILE - 1)
        b = (lax.broadcasted_iota(jnp.int32, full.shape, 1) >> lp) & (GROUPS_PER_TILE - 1)
        out_ref[t0 * rows:t0 * rows + n, :] = jnp.where(a == b, full, 0.0).astype(BF16)


def _s5_kernel(u_ref, wvc_ref, cpc_ref, sel_ref, ac_ref, d_ref, glu_ref, o_ref,
               mi_s, wv_s, cp_s, ubf_s, v_s, sp_s, y_s, ynat_s):
    T = S5_CHUNK
    nc = u_ref.shape[0] // T
    ns = ac_ref.shape[-1] // 2
    pair = MXU_DIM // LANES

    @pl.when(pl.program_id(1) == 0)
    def _():
        _expand_groups(wvc_ref, sel_ref, wv_s)
        _expand_groups(cpc_ref, sel_ref, cp_s)
        toe = lax.dot_general(wv_s[(T - 1) * LANES:T * LANES, :], cp_s[0:T * LANES, :],
                              (((1,), (1,)), ((), ())), preferred_element_type=F32).astype(BF16)
        for s in range(T):
            for t in range(T):
                if t >= s:
                    tile = toe[:, (t - s) * LANES:(t - s + 1) * LANES]
                elif t // pair == s // pair:
                    tile = jnp.zeros((LANES, LANES), BF16)
                else:
                    continue
                mi_s[s * LANES:(s + 1) * LANES, t * LANES:(t + 1) * LANES] = tile

    for t in range(T):
        ubf_s[:, t * LANES:(t + 1) * LANES] = u_ref[pl.ds(t, nc, stride=T), :].astype(BF16)
    v_s[...] = jnp.dot(ubf_s[...], wv_s[...], preferred_element_type=F32)
    ar, ai = ac_ref[0, :, 0:ns], ac_ref[0, :, ns:2 * ns]

    def step(c, carry):
        sr, si = carry
        sp_s[pl.ds(c, 1), 0:ns] = sr
        sp_s[pl.ds(c, 1), ns:2 * ns] = si
        vr = v_s[pl.ds(c, 1), 0:ns]
        vi = v_s[pl.ds(c, 1), ns:2 * ns]
        return ar * sr - ai * si + vr, ar * si + ai * sr + vi

    zero = jnp.zeros((1, ns), F32)
    lax.fori_loop(0, nc, step, (zero, zero))
    sp = sp_s[...].astype(BF16)
    for cb in range(T // pair):
        kk = (cb + 1) * MXU_DIM
        cols = slice(cb * MXU_DIM, (cb + 1) * MXU_DIM)
        wo_rows = slice(LANES + cb * MXU_DIM, LANES + (cb + 1) * MXU_DIM)
        y_s[:, cols] = (jnp.dot(ubf_s[:, 0:kk], mi_s[0:kk, cols], preferred_element_type=F32)
                        + lax.dot_general(sp, cp_s[wo_rows, :], (((1,), (1,)), ((), ())),
                                          preferred_element_type=F32))
    for t in range(T):
        ynat_s[pl.ds(t, nc, stride=T), :] = y_s[:, t * LANES:(t + 1) * LANES]
    y = ynat_s[...] + d_ref[...] * u_ref[...]
    yg = _gelu_tanh(y)
    gate = jnp.dot(yg.astype(BF16), glu_ref[0], preferred_element_type=F32)
    o_ref[...] = (yg * jax.nn.sigmoid(gate)).astype(o_ref.dtype)


def _s5(u, tables, sel, d, layer, batch, seq_len):
    wvc, cpc, a_chunk, glu = tables
    J = glu.shape[1]
    M, width = u.shape
    T = S5_CHUNK
    nc = seq_len // T
    n2 = a_chunk.shape[-1]
    per_tile = lambda x: pl.BlockSpec((None, 1) + x.shape[2:], lambda j, b: (layer, j) + (0,) * (x.ndim - 2))
    compact = lambda x: pl.BlockSpec((None, x.shape[1], None) + x.shape[3:], lambda j, b: (layer, 0, j, 0, 0))
    return pl.pallas_call(
        _s5_kernel,
        grid=(J, batch),
        in_specs=[pl.BlockSpec((seq_len, LANES), lambda j, b: (b, j)),
                  compact(wvc), compact(cpc),
                  pl.BlockSpec(sel.shape, lambda j, b: (0, 0)),
                  per_tile(a_chunk),
                  pl.BlockSpec((None, 1, LANES), lambda j, b: (layer, 0, j)),
                  per_tile(glu)],
        out_specs=pl.BlockSpec((seq_len, LANES), lambda j, b: (b, j)),
        out_shape=jax.ShapeDtypeStruct((M, width), BF16),
        scratch_shapes=[pltpu.VMEM((T * LANES, T * LANES), BF16),
                        pltpu.VMEM((T * LANES, n2), BF16),
                        pltpu.VMEM(((T + 1) * LANES, n2), BF16),
                        pltpu.VMEM((nc, T * LANES), BF16),
                        pltpu.VMEM((nc, n2), F32),
                        pltpu.VMEM((nc, n2), F32),
                        pltpu.VMEM((nc, T * LANES), F32),
                        pltpu.VMEM((seq_len, LANES), F32)],
        compiler_params=_params("arbitrary", "arbitrary"),
        name="s5",
    )(u, wvc, cpc, sel, a_chunk, d.reshape(d.shape[0], 1, width), glu)


def _t5_bucket_np(rel):
    max_exact = N_BUCKETS // 2
    n = np.maximum(rel, 0)
    nf = np.maximum(n, max_exact).astype(np.float32)
    large = max_exact + (np.log(nf / np.float32(max_exact)) / np.float32(math.log(MAX_DISTANCE / max_exact))
                         * np.float32(N_BUCKETS - max_exact)).astype(np.int32)
    large = np.minimum(large, N_BUCKETS - 1)
    return np.where(n < max_exact, n, large).astype(np.int32)


def _swa_table_kernel(bias_ref, sink_ref, bucket_ref, valid_ref, add_ref):
    bucket = bucket_ref[...]
    valid = valid_ref[0] != 0
    sink_col = lax.broadcasted_iota(jnp.int32, bucket.shape, 1) == 0

    @pl.loop(0, add_ref.shape[1])
    def _(h):
        acc = jnp.full(bucket.shape, NEG_INF, F32)
        for b in range(N_BUCKETS):
            acc = jnp.where(bucket == b, bias_ref[b, h], acc)
        acc = jnp.where(valid, acc, NEG_INF)
        add_ref[0, h] = jnp.where(sink_col, sink_ref[h], acc)


def _swa_table(rel_bias, sinks):
    blk = ATT_BLOCK
    qi = np.arange(blk)[:, None]
    kj = np.arange(2 * blk)[None, :]
    rel = qi + blk - kj
    in_window = (rel >= 0) & (rel < blk)
    valid = np.stack([in_window & (kj >= blk), in_window]).astype(np.int32)
    bucket = _t5_bucket_np(rel)
    return pl.pallas_call(
        _swa_table_kernel,
        grid=(2,),
        in_specs=[pl.BlockSpec(memory_space=pltpu.SMEM),
                  pl.BlockSpec(memory_space=pltpu.SMEM),
                  pl.BlockSpec((blk, 2 * blk), lambda v: (0, 0)),
                  pl.BlockSpec((1, blk, 2 * blk), lambda v: (v, 0, 0))],
        out_specs=pl.BlockSpec((1, N_Q_HEADS, blk, 2 * blk), lambda v: (v, 0, 0, 0)),
        out_shape=jax.ShapeDtypeStruct((2, N_Q_HEADS, blk, 2 * blk), F32),
        compiler_params=_params("arbitrary"),
        name="swa_table",
    )(rel_bias, sinks, jnp.asarray(bucket), jnp.asarray(valid))


def _swa_kernel(q_ref, kp_ref, kc_ref, vp_ref, vc_ref, add_ref, o_ref):
    blk, hd, g = ATT_BLOCK, HEAD_DIM, Q_PER_KV
    scale = HEAD_DIM ** -0.5
    assert math.frexp(scale)[0] == 0.5
    first_row = lax.broadcasted_iota(jnp.int32, (2 * blk, hd), 0) == 0
    outs = []
    for k in range(N_KV_HEADS):
        sl = slice(k * hd, (k + 1) * hd)
        keys = jnp.concatenate([kp_ref[:, sl], kc_ref[:, sl]], axis=0)
        vals = jnp.concatenate([vp_ref[:, sl], vc_ref[:, sl]], axis=0)
        keys = jnp.where(first_row, jnp.zeros_like(keys), keys)
        vals = jnp.where(first_row, jnp.zeros_like(vals), vals)
        q = jnp.concatenate([q_ref[:, (k * g + i) * hd:(k * g + i + 1) * hd] for i in range(g)],
                            axis=0) * scale
        s = lax.dot_general(q, keys, (((1,), (1,)), ((), ())), preferred_element_type=F32)
        s = s + add_ref[0, k * g:(k + 1) * g].reshape(g * blk, 2 * blk)
        m = jnp.max(s, axis=-1, keepdims=True)
        p = jnp.exp(s - m)
        den = jnp.sum(p, axis=-1, keepdims=True)
        o = jnp.dot(p.astype(BF16), vals, preferred_element_type=F32) / den
        outs.extend(o[i * blk:(i + 1) * blk] for i in range(g))
    o_ref[...] = jnp.concatenate(outs, axis=-1).astype(o_ref.dtype)


def _swa(z, add, batch, seq_len):
    blk = ATT_BLOCK
    nblk = seq_len // blk
    M = z.shape[0]
    dq = N_Q_HEADS * HEAD_DIM
    dkv = N_KV_HEADS * HEAD_DIM
    kcol, vcol = dq // dkv, dq // dkv + 1
    cur = lambda c: pl.BlockSpec((blk, dkv), lambda b, n: (b * nblk + n, c))
    prev = lambda c: pl.BlockSpec((blk, dkv), lambda b, n: (b * nblk + jnp.maximum(n - 1, 0), c))
    return pl.pallas_call(
        _swa_kernel,
        grid=(batch, nblk),
        in_specs=[pl.BlockSpec((blk, dq), lambda b, n: (b * nblk + n, 0)),
                  prev(kcol), cur(kcol), prev(vcol), cur(vcol),
                  pl.BlockSpec((1, N_Q_HEADS, blk, 2 * blk), lambda b, n: (jnp.minimum(n, 1), 0, 0, 0))],
        out_specs=pl.BlockSpec((blk, dq), lambda b, n: (b * nblk + n, 0)),
        out_shape=jax.ShapeDtypeStruct((M, dq), BF16),
        compiler_params=_params("arbitrary", "arbitrary"),
        name="swa",
    )(z, z, z, z, z, add)


def _xattn_kernel(rc, hn_ref, h_ref, kv_ref, wq_ref, wo_ref, g_ref, hout_ref, hnout_ref):
    d = hn_ref.shape[1]
    hd = d // X_HEADS
    g = g_ref[...]
    for c in range(hn_ref.shape[0] // rc):
        rows = slice(c * rc, (c + 1) * rc)
        q = jnp.dot(hn_ref[rows, :], wq_ref[...], preferred_element_type=F32).astype(BF16)
        outs = []
        for h in range(X_HEADS):
            k = kv_ref[:, h * hd:(h + 1) * hd]
            v = kv_ref[:, d + h * hd:d + (h + 1) * hd]
            s = lax.dot_general(q[:, h * hd:(h + 1) * hd], k, (((1,), (1,)), ((), ())),
                                preferred_element_type=F32) * (hd ** -0.5)
            m = jnp.max(s, axis=-1, keepdims=True)
            p = jnp.exp(s - m)
            den = jnp.sum(p, axis=-1, keepdims=True)
            outs.append((jnp.dot(p.astype(BF16), v, preferred_element_type=F32) / den).astype(BF16))
        o = jnp.concatenate(outs, axis=-1)
        acc = h_ref[rows, :] + jnp.dot(o, wo_ref[...], preferred_element_type=F32)
        hout_ref[rows, :] = acc
        hnout_ref[rows, :] = _rms(acc, g).astype(hnout_ref.dtype)


def _xattn_block(hn, h, kv, w_q, w_o, layer, g, batch, seq_len, n_mem, *, tq=512):
    M, d = hn.shape
    tq = min(tq, seq_len)
    rc = min(ROW_CHUNK, tq // 2)
    nq = seq_len // tq
    row = pl.BlockSpec((tq, d), lambda b, i: (b * nq + i, 0))
    wspec = pl.BlockSpec((None, d, d), lambda b, i: (layer, 0, 0), pipeline_mode=pl.Buffered(1))
    return pl.pallas_call(
        functools.partial(_xattn_kernel, rc),
        grid=(batch, nq),
        in_specs=[row, row,
                  pl.BlockSpec((n_mem, 2 * d), lambda b, i: (b, 0)),
                  wspec, wspec,
                  pl.BlockSpec((1, d), lambda b, i: (0, 0))],
        out_specs=[row, row],
        out_shape=[jax.ShapeDtypeStruct((M, d), F32), jax.ShapeDtypeStruct((M, d), BF16)],
        compiler_params=_params("arbitrary", "arbitrary"),
        name="xattn",
    )(hn, h, kv, w_q, w_o, g.reshape(1, d))


def kernel(x, mem, norm_mix, norm_xattn, norm_ffn, norm_final, norm_mem, rel_bias, ev_w_in, ev_conv_w, s5_a_re, s5_a_im, s5_log_dt, s5_b_re, s5_b_im, s5_c_re, s5_c_im, s5_d, s5_glu_w, ev_w_out, od_w_qkv, od_b_qkv, od_sinks, od_w_out, xa_w_q, xa_w_kv, xa_w_o, ff_w_gate, ff_w_up, ff_conv_w, ff_conv_b, ff_w_down):
    batch, seq_len, d = x.shape
    n_mem = mem.shape[1]
    depth = norm_mix.shape[0]
    M = batch * seq_len
    a_width = ev_conv_w.shape[-1]
    ev_w_out, od_w_out, xa_w_q, xa_w_o = (w.astype(BF16) for w in (ev_w_out, od_w_out, xa_w_q, xa_w_o))

    s5_tables = jax.vmap(_s5_tables)(s5_a_re, s5_a_im, s5_log_dt, s5_b_re, s5_b_im, s5_c_re, s5_c_im, s5_glu_w)
    s5_selector = _s5_selector(s5_a_re.shape[-1])

    mem_n = _rmsnorm(mem.reshape(batch * n_mem, d), norm_mem)
    h = x.reshape(M, d)
    hn = _rmsnorm(h, norm_mix[0])
    for l in range(depth):
        i = l // 2
        if l % 2 == 0:
            ya = _mixer_a(hn, ev_w_in, i, ev_conv_w, seq_len)
            u = _mm(hn, ev_w_in, i, col0=3 * a_width, out_dtype=F32, name="s5_in")
            ys = _s5(u, s5_tables, s5_selector, s5_d, i, batch, seq_len)
            h, hn = _mm_res_norm([ya, ys], ev_w_out, i, h, norm_xattn[l], name="ev_out")
        else:
            z = _mm(hn, od_w_qkv, i, bias=od_b_qkv, tn=1280, name="qkv")
            o = _swa(z, _swa_table(rel_bias, od_sinks[i]), batch, seq_len)
            h, hn = _mm_res_norm([o], od_w_out, i, h, norm_xattn[l], name="od_out")
        kv = _mm(mem_n, xa_w_kv, l, name="xa_kv")
        h, hn = _xattn_block(hn, h, kv, xa_w_q, xa_w_o, l, norm_ffn[l], batch, seq_len, n_mem)
        act, w_down = _ffn_up(hn, ff_w_gate, ff_w_up, ff_w_down, l, ff_conv_w, ff_conv_b, seq_len)
        if l + 1 < depth:
            h, hn = _mm_res_norm([act], w_down, 0, h, norm_mix[l + 1], name="ff_down")
        else:
            _, out = _mm_res_norm([act], w_down, 0, h, norm_final, emit_h=False, norm_dtype=F32,
                                  name="ff_down_final")
    return out.reshape(batch, seq_len, d)
```

```python
import functools
import math

import numpy as np
import jax
import jax.numpy as jnp
from jax import lax
from jax.experimental import pallas as pl
from jax.experimental.pallas import tpu as pltpu

F32 = jnp.float32
BF16 = jnp.bfloat16

RMS_EPS = 1e-5
NEG_INF = -1e30

HEAD_DIM = 64
Q_PER_KV = 8
N_KV_HEADS = 4
N_Q_HEADS = Q_PER_KV * N_KV_HEADS
ATT_BLOCK = 128
N_BUCKETS = 32
MAX_DISTANCE = 128
X_HEADS = 4
S5_GROUP = 16

LANES = 128
SUBLANES = 8
MXU_DIM = 256
VMEM_LIMIT_BYTES = 60000 * 1024

S5_CHUNK = 16
GROUPS_PER_TILE = LANES // S5_GROUP

ROW_CHUNK = 256


def _params(*sem):
    return pltpu.CompilerParams(dimension_semantics=sem, vmem_limit_bytes=VMEM_LIMIT_BYTES)


def _rms(x, g):
    ms = jnp.mean(x * x, axis=-1, keepdims=True)
    return x * lax.rsqrt(ms + RMS_EPS) * g


def _rmsnorm_kernel(x_ref, g_ref, o_ref):
    o_ref[...] = _rms(x_ref[...], g_ref[...]).astype(o_ref.dtype)


def _rmsnorm(x, g, *, tm=512):
    M, D = x.shape
    tm = min(tm, M)
    return pl.pallas_call(
        _rmsnorm_kernel,
        grid=(M // tm,),
        in_specs=[pl.BlockSpec((tm, D), lambda i: (i, 0)), pl.BlockSpec((1, D), lambda i: (0, 0))],
        out_specs=pl.BlockSpec((tm, D), lambda i: (i, 0)),
        out_shape=jax.ShapeDtypeStruct((M, D), BF16),
        compiler_params=_params("arbitrary"),
        name="rmsnorm",
    )(x, g.reshape(1, D))


def _mm_kernel(has_bias, rc, a_ref, w_ref, *rest):
    if has_bias:
        b_ref, o_ref, wbf_ref = rest
    else:
        o_ref, wbf_ref = rest

    @pl.when(pl.program_id(1) == 0)
    def _():
        wbf_ref[...] = w_ref[...].astype(BF16)

    for c in range(a_ref.shape[0] // rc):
        rows = slice(c * rc, (c + 1) * rc)
        acc = jnp.dot(a_ref[rows, :], wbf_ref[...], preferred_element_type=F32)
        if has_bias:
            acc = acc + b_ref[...]
        o_ref[rows, :] = acc.astype(o_ref.dtype)


def _mm(a, w, layer, *, col0=0, n=None, bias=None, out_dtype=BF16, tm=2048, tn=1024, name="mm"):
    M, K = a.shape
    n = w.shape[2] - col0 if n is None else n
    tm, tn = min(tm, M), min(tn, n)
    rc = min(2 * ROW_CHUNK, tm)
    assert M % tm == 0 and n % tn == 0 and col0 % tn == 0 and tm % rc == 0
    cb = col0 // tn
    in_specs = [pl.BlockSpec((tm, K), lambda j, i: (i, 0)),
                pl.BlockSpec((None, K, tn), lambda j, i: (layer, 0, cb + j))]
    args = [a, w]
    if bias is not None:
        in_specs.append(pl.BlockSpec((None, 1, tn), lambda j, i: (layer, 0, cb + j)))
        args.append(bias.reshape(bias.shape[0], 1, -1))
    return pl.pallas_call(
        functools.partial(_mm_kernel, bias is not None, rc),
        grid=(n // tn, M // tm),
        in_specs=in_specs,
        out_specs=pl.BlockSpec((tm, tn), lambda j, i: (i, j)),
        out_shape=jax.ShapeDtypeStruct((M, n), out_dtype),
        scratch_shapes=[pltpu.VMEM((K, tn), BF16)],
        compiler_params=_params("arbitrary", "arbitrary"),
        name=name,
    )(*args)


def _mm_res_norm_kernel(emit_h, rc, n_a, *refs):
    a_refs = refs[:n_a]
    w_ref, h_ref, g_ref = refs[n_a:n_a + 3]
    outs = refs[n_a + 3:]
    hn_ref = outs[-1]
    g = g_ref[...]
    for c in range(h_ref.shape[0] // rc):
        rows = slice(c * rc, (c + 1) * rc)
        acc = h_ref[rows, :]
        k0 = 0
        for a_ref in a_refs:
            k1 = k0 + a_ref.shape[1]
            acc = acc + jnp.dot(a_ref[rows, :], w_ref[k0:k1, :], preferred_element_type=F32)
            k0 = k1
        if emit_h:
            outs[0][rows, :] = acc
        hn_ref[rows, :] = _rms(acc, g).astype(hn_ref.dtype)


def _mm_res_norm(a_parts, w, layer, h, g, *, emit_h=True, norm_dtype=BF16, tm=512, name="mm_res_norm"):
    M, D = h.shape
    assert sum(a.shape[1] for a in a_parts) == w.shape[1]
    tm = min(tm, M)
    rc = min(ROW_CHUNK, tm // 2)
    assert M % tm == 0 and tm % rc == 0
    row = pl.BlockSpec((tm, D), lambda i: (i, 0))
    out_shape = [jax.ShapeDtypeStruct((M, D), norm_dtype)]
    out_specs = [row]
    if emit_h:
        out_shape.insert(0, jax.ShapeDtypeStruct((M, D), F32))
        out_specs.insert(0, row)
    res = pl.pallas_call(
        functools.partial(_mm_res_norm_kernel, emit_h, rc, len(a_parts)),
        grid=(M // tm,),
        in_specs=[pl.BlockSpec((tm, a.shape[1]), lambda i: (i, 0)) for a in a_parts] + [
            pl.BlockSpec((None,) + w.shape[1:], lambda i: (layer, 0, 0), pipeline_mode=pl.Buffered(1)),
            row,
            pl.BlockSpec((1, D), lambda i: (0, 0))],
        out_specs=out_specs,
        out_shape=out_shape,
        compiler_params=_params("arbitrary"),
        name=name,
    )(*a_parts, w, h, g.reshape(1, D))
    return res if emit_h else (None, res[0])


def _conv3_rows(p, prev8, w_ref):
    w0, w1, w2 = w_ref[0:1, :], w_ref[1:2, :], w_ref[2:3, :]
    full = w2 * p + w1 * pltpu.roll(p, 1, 0) + w0 * pltpu.roll(p, 2, 0)
    e = jnp.concatenate([prev8, p[0:SUBLANES]], axis=0)
    f = w2 * e + w1 * pltpu.roll(e, 1, 0) + w0 * pltpu.roll(e, 2, 0)
    return full, f[SUBLANES:2 * SUBLANES]


def _load_halo(halo_ref, tiles_per_seq):
    @pl.when(pl.program_id(1) % tiles_per_seq == 0)
    def _():
        halo_ref[...] = jnp.zeros_like(halo_ref)
    return halo_ref[...]


def _mixer_a_kernel(tiles_per_seq, rc, a_ref, wb_ref, wc_ref, wx_ref, cw_ref, o_ref,
                    wb_s, wc_s, wx_s, halo_s):
    @pl.when(pl.program_id(1) == 0)
    def _():
        wb_s[...] = wb_ref[...].astype(BF16)
        wc_s[...] = wc_ref[...].astype(BF16)
        wx_s[...] = wx_ref[...].astype(BF16)

    prev8 = _load_halo(halo_s, tiles_per_seq)
    for c in range(a_ref.shape[0] // rc):
        r0 = c * rc
        a = a_ref[r0:r0 + rc, :]
        gb = jnp.dot(a, wb_s[...], preferred_element_type=F32)
        p = (jnp.dot(a, wc_s[...], preferred_element_type=F32)
             * jnp.dot(a, wx_s[...], preferred_element_type=F32))
        full, first8 = _conv3_rows(p, prev8, cw_ref)
        prev8 = p[rc - SUBLANES:rc]
        o_ref[r0:r0 + rc, :] = (gb * full).astype(o_ref.dtype)
        o_ref[r0:r0 + SUBLANES, :] = (gb[0:SUBLANES] * first8).astype(o_ref.dtype)
    halo_s[...] = prev8


def _mixer_a(hn, w_in, layer, conv_w, seq_len, *, tm=2048, tn=512):
    M, K = hn.shape
    ca = conv_w.shape[2]
    tm, tn = min(tm, seq_len), min(tn, ca)
    rc = min(ROW_CHUNK, tm)
    assert seq_len % tm == 0 and ca % tn == 0 and tm % rc == 0
    nj = ca // tn
    wspec = lambda off: pl.BlockSpec((None, K, tn), lambda j, i: (layer, 0, off * nj + j))
    return pl.pallas_call(
        functools.partial(_mixer_a_kernel, seq_len // tm, rc),
        grid=(nj, M // tm),
        in_specs=[pl.BlockSpec((tm, K), lambda j, i: (i, 0)),
                  wspec(0), wspec(1), wspec(2),
                  pl.BlockSpec((None, 3, tn), lambda j, i: (layer, 0, j))],
        out_specs=pl.BlockSpec((tm, tn), lambda j, i: (i, j)),
        out_shape=jax.ShapeDtypeStruct((M, ca), BF16),
        scratch_shapes=[pltpu.VMEM((K, tn), BF16)] * 3 + [pltpu.VMEM((SUBLANES, tn), F32)],
        compiler_params=_params("arbitrary", "arbitrary"),
        name="mixer_a",
    )(hn, w_in, w_in, w_in, conv_w)


def _ffn_up_kernel(tiles_per_seq, rc, a_ref, wg_ref, wu_ref, cw_ref, cb_ref, wd_ref, o_ref, wd_bf_ref,
                   wg_s, wu_s, halo_s):
    @pl.when(pl.program_id(1) == 0)
    def _():
        wg_s[...] = wg_ref[...].astype(BF16)
        wu_s[...] = wu_ref[...].astype(BF16)
        wd_bf_ref[...] = wd_ref[...].astype(BF16)

    b = cb_ref[...]
    prev8 = _load_halo(halo_s, tiles_per_seq)
    for c in range(a_ref.shape[0] // rc):
        r0 = c * rc
        a = a_ref[r0:r0 + rc, :]
        gate = jnp.dot(a, wg_s[...], preferred_element_type=F32)
        up = jnp.dot(a, wu_s[...], preferred_element_type=F32)
        full, first8 = _conv3_rows(gate, prev8, cw_ref)
        prev8 = gate[rc - SUBLANES:rc]
        o_ref[r0:r0 + rc, :] = (jax.nn.silu(full + b) * up).astype(o_ref.dtype)
        o_ref[r0:r0 + SUBLANES, :] = (jax.nn.silu(first8 + b) * up[0:SUBLANES]).astype(o_ref.dtype)
    halo_s[...] = prev8


def _ffn_up(hn, w_gate, w_up, w_down, layer, conv_w, conv_b, seq_len, *, tm=2048, tn=512):
    M, K = hn.shape
    dff, d_out = w_down.shape[1:]
    tm, tn = min(tm, seq_len), min(tn, dff)
    rc = min(ROW_CHUNK, tm)
    assert seq_len % tm == 0 and dff % tn == 0 and tm % rc == 0
    wspec = pl.BlockSpec((None, K, tn), lambda j, i: (layer, 0, j))
    return pl.pallas_call(
        functools.partial(_ffn_up_kernel, seq_len // tm, rc),
        grid=(dff // tn, M // tm),
        in_specs=[pl.BlockSpec((tm, K), lambda j, i: (i, 0)),
                  wspec, wspec,
                  pl.BlockSpec((None, 3, tn), lambda j, i: (layer, 0, j)),
                  pl.BlockSpec((None, 1, tn), lambda j, i: (layer, 0, j)),
                  pl.BlockSpec((None, tn, d_out), lambda j, i: (layer, j, 0))],
        out_specs=[pl.BlockSpec((tm, tn), lambda j, i: (i, j)),
                   pl.BlockSpec((None, tn, d_out), lambda j, i: (0, j, 0))],
        out_shape=[jax.ShapeDtypeStruct((M, dff), BF16), jax.ShapeDtypeStruct((1, dff, d_out), BF16)],
        scratch_shapes=[pltpu.VMEM((K, tn), BF16)] * 2 + [pltpu.VMEM((SUBLANES, tn), F32)],
        compiler_params=_params("arbitrary", "arbitrary"),
        name="ffn_up",
    )(hn, w_gate, w_up, conv_w, conv_b.reshape(conv_b.shape[0], 1, dff), w_down)


def _log2(n):
    assert n & (n - 1) == 0
    return n.bit_length() - 1


def _s5_selector(P):
    sel = np.kron(np.eye(2, dtype=np.float32), np.tile(np.eye(P, dtype=np.float32), (1, GROUPS_PER_TILE)))
    return jnp.asarray(sel, BF16)


def _s5_tables(a_re, a_im, log_dt, b_re, b_im, c_re, c_im, glu_w):
    T, gt = S5_CHUNK, GROUPS_PER_TILE
    G, P = a_re.shape
    H = b_re.shape[-1]
    J = G // gt
    dt = jnp.exp(log_dt)[:, None]
    lr, li = a_re * dt, a_im * dt

    def a_pow(taus):
        tau = jnp.asarray(np.asarray(taus, np.float32))[:, None, None]
        mag = jnp.exp(tau * lr)
        return mag * jnp.cos(tau * li), mag * jnp.sin(tau * li)

    pw_re, pw_im = a_pow(np.arange(T + 1))
    nr, ni = pw_re[1] - 1.0, pw_im[1]
    den = a_re * a_re + a_im * a_im
    qr, qi = (nr * a_re + ni * a_im) / den, (ni * a_re - nr * a_im) / den
    bt_re, bt_im = b_re.transpose(0, 2, 1), b_im.transpose(0, 2, 1)
    bb_re = qr[:, None, :] * bt_re - qi[:, None, :] * bt_im
    bb_im = qr[:, None, :] * bt_im + qi[:, None, :] * bt_re
    cp_re = c_re[None] * pw_re[:, :, None, :] - c_im[None] * pw_im[:, :, None, :]
    cp_im = c_re[None] * pw_im[:, :, None, :] + c_im[None] * pw_re[:, :, None, :]
    rev_re, rev_im = a_pow(T - 1 - np.arange(T))
    wv_re = rev_re[:, :, None, :] * bb_re[None] - rev_im[:, :, None, :] * bb_im[None]
    wv_im = rev_re[:, :, None, :] * bb_im[None] + rev_im[:, :, None, :] * bb_re[None]

    def tiles(compact):
        full = jnp.dot(compact.astype(BF16), jnp.asarray(np.tile(np.eye(H, dtype=np.float32), (1, gt)), BF16),
                       preferred_element_type=F32)
        rg = lax.broadcasted_iota(jnp.int32, (LANES, LANES), 0) // H
        cg = lax.broadcasted_iota(jnp.int32, (LANES, LANES), 1) // H
        return jnp.where(rg == cg, full, 0.0).astype(BF16)

    glu = tiles(glu_w.reshape(J, gt * H, H))

    wvc = jnp.concatenate([wv_re, wv_im], axis=-1).astype(BF16).reshape(T, J, gt * H, 2 * P)
    cpc = jnp.concatenate([cp_re, -cp_im], axis=-1).astype(BF16).reshape(T + 1, J, gt * H, 2 * P)
    a_chunk = jnp.concatenate([pw_re[T].reshape(J, 1, gt * P), pw_im[T].reshape(J, 1, gt * P)], axis=-1)
    return wvc, cpc, a_chunk, glu


def _gelu_tanh(x):
    c = math.sqrt(2.0 / math.pi)
    return 0.5 * x * (1.0 + jnp.tanh(c * (x + 0.044715 * (x * x * x))))


def _expand_groups(compact_ref, sel_ref, out_ref, steps_per_dot=4):
    nblk, rows, _ = compact_ref.shape
    lh, lp = _log2(S5_GROUP), _log2(out_ref.shape[1] // (2 * GROUPS_PER_TILE))
    for t0 in range(0, nblk, steps_per_dot):
        n = rows * min(steps_per_dot, nblk - t0)
        full = jnp.dot(compact_ref[t0:t0 + n // rows].reshape(n, -1), sel_ref[...],
                       preferred_element_type=F32)
        a = (lax.broadcasted_iota(jnp.int32, full.shape, 0) >> lh) & (GROUPS_PER_TILE - 1)
        b = (lax.broadcasted_iota(jnp.int32, full.shape, 1) >> lp) & (GROUPS_PER_TILE - 1)
        out_ref[t0 * rows:t0 * rows + n, :] = jnp.where(a == b, full, 0.0).astype(BF16)


def _s5_kernel(u_ref, wvc_ref, cpc_ref, sel_ref, ac_ref, d_ref, glu_ref, o_ref,
               mi_s, wv_s, cp_s, ubf_s, v_s, sp_s, y_s, ynat_s):
    T = S5_CHUNK
    nc = u_ref.shape[0] // T
    ns = ac_ref.shape[-1] // 2
    pair = MXU_DIM // LANES

    @pl.when(pl.program_id(1) == 0)
    def _():
        _expand_groups(wvc_ref, sel_ref, wv_s)
        _expand_groups(cpc_ref, sel_ref, cp_s)
        toe = lax.dot_general(wv_s[(T - 1) * LANES:T * LANES, :], cp_s[0:T * LANES, :],
                              (((1,), (1,)), ((), ())), preferred_element_type=F32).astype(BF16)
        for s in range(T):
            for t in range(T):
                if t >= s:
                    tile = toe[:, (t - s) * LANES:(t - s + 1) * LANES]
                elif t // pair == s // pair:
                    tile = jnp.zeros((LANES, LANES), BF16)
                else:
                    continue
                mi_s[s * LANES:(s + 1) * LANES, t * LANES:(t + 1) * LANES] = tile

    for t in range(T):
        ubf_s[:, t * LANES:(t + 1) * LANES] = u_ref[pl.ds(t, nc, stride=T), :].astype(BF16)
    v_s[...] = jnp.dot(ubf_s[...], wv_s[...], preferred_element_type=F32)
    ar, ai = ac_ref[0, :, 0:ns], ac_ref[0, :, ns:2 * ns]

    def step(c, carry):
        sr, si = carry
        sp_s[pl.ds(c, 1), 0:ns] = sr
        sp_s[pl.ds(c, 1), ns:2 * ns] = si
        vr = v_s[pl.ds(c, 1), 0:ns]
        vi = v_s[pl.ds(c, 1), ns:2 * ns]
        return ar * sr - ai * si + vr, ar * si + ai * sr + vi

    zero = jnp.zeros((1, ns), F32)
    lax.fori_loop(0, nc, step, (zero, zero))
    sp = sp_s[...].astype(BF16)
    for cb in range(T // pair):
        kk = (cb + 1) * MXU_DIM
        cols = slice(cb * MXU_DIM, (cb + 1) * MXU_DIM)
        wo_rows = slice(LANES + cb * MXU_DIM, LANES + (cb + 1) * MXU_DIM)
        y_s[:, cols] = (jnp.dot(ubf_s[:, 0:kk], mi_s[0:kk, cols], preferred_element_type=F32)
                        + lax.dot_general(sp, cp_s[wo_rows, :], (((1,), (1,)), ((), ())),
                                          preferred_element_type=F32))
    for t in range(T):
        ynat_s[pl.ds(t, nc, stride=T), :] = y_s[:, t * LANES:(t + 1) * LANES]
    y = ynat_s[...] + d_ref[...] * u_ref[...]
    yg = _gelu_tanh(y)
    gate = jnp.dot(yg.astype(BF16), glu_ref[0], preferred_element_type=F32)
    o_ref[...] = (yg * jax.nn.sigmoid(gate)).astype(o_ref.dtype)


def _s5(u, tables, sel, d, layer, batch, seq_len):
    wvc, cpc, a_chunk, glu = tables
    J = glu.shape[1]
    M, width = u.shape
    T = S5_CHUNK
    nc = seq_len // T
    n2 = a_chunk.shape[-1]
    per_tile = lambda x: pl.BlockSpec((None, 1) + x.shape[2:], lambda j, b: (layer, j) + (0,) * (x.ndim - 2))
    compact = lambda x: pl.BlockSpec((None, x.shape[1], None) + x.shape[3:], lambda j, b: (layer, 0, j, 0, 0))
    return pl.pallas_call(
        _s5_kernel,
        grid=(J, batch),
        in_specs=[pl.BlockSpec((seq_len, LANES), lambda j, b: (b, j)),
                  compact(wvc), compact(cpc),
                  pl.BlockSpec(sel.shape, lambda j, b: (0, 0)),
                  per_tile(a_chunk),
                  pl.BlockSpec((None, 1, LANES), lambda j, b: (layer, 0, j)),
                  per_tile(glu)],
        out_specs=pl.BlockSpec((seq_len, LANES), lambda j, b: (b, j)),
        out_shape=jax.ShapeDtypeStruct((M, width), BF16),
        scratch_shapes=[pltpu.VMEM((T * LANES, T * LANES), BF16),
                        pltpu.VMEM((T * LANES, n2), BF16),
                        pltpu.VMEM(((T + 1) * LANES, n2), BF16),
                        pltpu.VMEM((nc, T * LANES), BF16),
                        pltpu.VMEM((nc, n2), F32),
                        pltpu.VMEM((nc, n2), F32),
                        pltpu.VMEM((nc, T * LANES), F32),
                        pltpu.VMEM((seq_len, LANES), F32)],
        compiler_params=_params("arbitrary", "arbitrary"),
        name="s5",
    )(u, wvc, cpc, sel, a_chunk, d.reshape(d.shape[0], 1, width), glu)


def _t5_bucket_np(rel):
    max_exact = N_BUCKETS // 2
    n = np.maximum(rel, 0)
    nf = np.maximum(n, max_exact).astype(np.float32)
    large = max_exact + (np.log(nf / np.float32(max_exact)) / np.float32(math.log(MAX_DISTANCE / max_exact))
                         * np.float32(N_BUCKETS - max_exact)).astype(np.int32)
    large = np.minimum(large, N_BUCKETS - 1)
    return np.where(n < max_exact, n, large).astype(np.int32)


def _swa_table_kernel(bias_ref, sink_ref, bucket_ref, valid_ref, add_ref):
    bucket = bucket_ref[...]
    valid = valid_ref[0] != 0
    sink_col = lax.broadcasted_iota(jnp.int32, bucket.shape, 1) == 0

    @pl.loop(0, add_ref.shape[1])
    def _(h):
        acc = jnp.full(bucket.shape, NEG_INF, F32)
        for b in range(N_BUCKETS):
            acc = jnp.where(bucket == b, bias_ref[b, h], acc)
        acc = jnp.where(valid, acc, NEG_INF)
        add_ref[0, h] = jnp.where(sink_col, sink_ref[h], acc)


def _swa_table(rel_bias, sinks):
    blk = ATT_BLOCK
    qi = np.arange(blk)[:, None]
    kj = np.arange(2 * blk)[None, :]
    rel = qi + blk - kj
    in_window = (rel >= 0) & (rel < blk)
    valid = np.stack([in_window & (kj >= blk), in_window]).astype(np.int32)
    bucket = _t5_bucket_np(rel)
    return pl.pallas_call(
        _swa_table_kernel,
        grid=(2,),
        in_specs=[pl.BlockSpec(memory_space=pltpu.SMEM),
                  pl.BlockSpec(memory_space=pltpu.SMEM),
                  pl.BlockSpec((blk, 2 * blk), lambda v: (0, 0)),
                  pl.BlockSpec((1, blk, 2 * blk), lambda v: (v, 0, 0))],
        out_specs=pl.BlockSpec((1, N_Q_HEADS, blk, 2 * blk), lambda v: (v, 0, 0, 0)),
        out_shape=jax.ShapeDtypeStruct((2, N_Q_HEADS, blk, 2 * blk), F32),
        compiler_params=_params("arbitrary"),
        name="swa_table",
    )(rel_bias, sinks, jnp.asarray(bucket), jnp.asarray(valid))


def _swa_kernel(q_ref, kp_ref, kc_ref, vp_ref, vc_ref, add_ref, o_ref):
    blk, hd, g = ATT_BLOCK, HEAD_DIM, Q_PER_KV
    scale = HEAD_DIM ** -0.5
    assert math.frexp(scale)[0] == 0.5
    first_row = lax.broadcasted_iota(jnp.int32, (2 * blk, hd), 0) == 0
    outs = []
    for k in range(N_KV_HEADS):
        sl = slice(k * hd, (k + 1) * hd)
        keys = jnp.concatenate([kp_ref[:, sl], kc_ref[:, sl]], axis=0)
        vals = jnp.concatenate([vp_ref[:, sl], vc_ref[:, sl]], axis=0)
        keys = jnp.where(first_row, jnp.zeros_like(keys), keys)
        vals = jnp.where(first_row, jnp.zeros_like(vals), vals)
        q = jnp.concatenate([q_ref[:, (k * g + i) * hd:(k * g + i + 1) * hd] for i in range(g)],
                            axis=0) * scale
        s = lax.dot_general(q, keys, (((1,), (1,)), ((), ())), preferred_element_type=F32)
        s = s + add_ref[0, k * g:(k + 1) * g].reshape(g * blk, 2 * blk)
        m = jnp.max(s, axis=-1, keepdims=True)
        p = jnp.exp(s - m)
        den = jnp.sum(p, axis=-1, keepdims=True)
        o = jnp.dot(p.astype(BF16), vals, preferred_element_type=F32) / den
        outs.extend(o[i * blk:(i + 1) * blk] for i in range(g))
    o_ref[...] = jnp.concatenate(outs, axis=-1).astype(o_ref.dtype)


def _swa(z, add, batch, seq_len):
    blk = ATT_BLOCK
    nblk = seq_len // blk
    M = z.shape[0]
    dq = N_Q_HEADS * HEAD_DIM
    dkv = N_KV_HEADS * HEAD_DIM
    kcol, vcol = dq // dkv, dq // dkv + 1
    cur = lambda c: pl.BlockSpec((blk, dkv), lambda b, n: (b * nblk + n, c))
    prev = lambda c: pl.BlockSpec((blk, dkv), lambda b, n: (b * nblk + jnp.maximum(n - 1, 0), c))
    return pl.pallas_call(
        _swa_kernel,
        grid=(batch, nblk),
        in_specs=[pl.BlockSpec((blk, dq), lambda b, n: (b * nblk + n, 0)),
                  prev(kcol), cur(kcol), prev(vcol), cur(vcol),
                  pl.BlockSpec((1, N_Q_HEADS, blk, 2 * blk), lambda b, n: (jnp.minimum(n, 1), 0, 0, 0))],
        out_specs=pl.BlockSpec((blk, dq), lambda b, n: (b * nblk + n, 0)),
        out_shape=jax.ShapeDtypeStruct((M, dq), BF16),
        compiler_params=_params("arbitrary", "arbitrary"),
        name="swa",
    )(z, z, z, z, z, add)


def _out_xattn_kernel(rc, n_a, *refs):
    a_refs = refs[:n_a]
    w_ref, h_ref, g1_ref, kv_ref, wq_ref, wo_ref, g2_ref, hout_ref, hnout_ref = refs[n_a:]
    d = h_ref.shape[1]
    hd = d // X_HEADS
    g1, g2 = g1_ref[...], g2_ref[...]
    for c in range(h_ref.shape[0] // rc):
        rows = slice(c * rc, (c + 1) * rc)
        acc = h_ref[rows, :]
        k0 = 0
        for a_ref in a_refs:
            k1 = k0 + a_ref.shape[1]
            acc = acc + jnp.dot(a_ref[rows, :], w_ref[k0:k1, :], preferred_element_type=F32)
            k0 = k1
        hn = _rms(acc, g1).astype(BF16)
        q = jnp.dot(hn, wq_ref[...], preferred_element_type=F32).astype(BF16)
        outs = []
        for h in range(X_HEADS):
            k = kv_ref[:, h * hd:(h + 1) * hd]
            v = kv_ref[:, d + h * hd:d + (h + 1) * hd]
            s = lax.dot_general(q[:, h * hd:(h + 1) * hd], k, (((1,), (1,)), ((), ())),
                                preferred_element_type=F32) * (hd ** -0.5)
            m = jnp.max(s, axis=-1, keepdims=True)
            p = jnp.exp(s - m)
            den = jnp.sum(p, axis=-1, keepdims=True)
            outs.append((jnp.dot(p.astype(BF16), v, preferred_element_type=F32) / den).astype(BF16))
        acc = acc + jnp.dot(jnp.concatenate(outs, axis=-1), wo_ref[...], preferred_element_type=F32)
        hout_ref[rows, :] = acc
        hnout_ref[rows, :] = _rms(acc, g2).astype(hnout_ref.dtype)


def _out_xattn(a_parts, w_out, layer_out, h, g1, kv, w_q, w_o, layer, g2, batch, seq_len, n_mem, *, tq=512):
    M, d = h.shape
    tq = min(tq, seq_len)
    rc = min(ROW_CHUNK, tq // 2)
    nq = seq_len // tq
    row = pl.BlockSpec((tq, d), lambda b, i: (b * nq + i, 0))
    gain = pl.BlockSpec((1, d), lambda b, i: (0, 0))
    resident = lambda w, l: pl.BlockSpec((None,) + w.shape[1:], lambda b, i: (l, 0, 0),
                                         pipeline_mode=pl.Buffered(1))
    return pl.pallas_call(
        functools.partial(_out_xattn_kernel, rc, len(a_parts)),
        grid=(batch, nq),
        in_specs=[pl.BlockSpec((tq, a.shape[1]), lambda b, i: (b * nq + i, 0)) for a in a_parts] + [
            resident(w_out, layer_out), row, gain,
            pl.BlockSpec((n_mem, 2 * d), lambda b, i: (b, 0)),
            resident(w_q, layer), resident(w_o, layer), gain],
        out_specs=[row, row],
        out_shape=[jax.ShapeDtypeStruct((M, d), F32), jax.ShapeDtypeStruct((M, d), BF16)],
        compiler_params=_params("arbitrary", "arbitrary"),
        name="out_xattn",
    )(*a_parts, w_out, h, g1.reshape(1, d), kv, w_q, w_o, g2.reshape(1, d))


def kernel(x, mem, norm_mix, norm_xattn, norm_ffn, norm_final, norm_mem, rel_bias, ev_w_in, ev_conv_w, s5_a_re, s5_a_im, s5_log_dt, s5_b_re, s5_b_im, s5_c_re, s5_c_im, s5_d, s5_glu_w, ev_w_out, od_w_qkv, od_b_qkv, od_sinks, od_w_out, xa_w_q, xa_w_kv, xa_w_o, ff_w_gate, ff_w_up, ff_conv_w, ff_conv_b, ff_w_down):
    batch, seq_len, d = x.shape
    n_mem = mem.shape[1]
    depth = norm_mix.shape[0]
    M = batch * seq_len
    a_width = ev_conv_w.shape[-1]
    ev_w_out, od_w_out, xa_w_q, xa_w_o = (w.astype(BF16) for w in (ev_w_out, od_w_out, xa_w_q, xa_w_o))

    s5_tables = jax.vmap(_s5_tables)(s5_a_re, s5_a_im, s5_log_dt, s5_b_re, s5_b_im, s5_c_re, s5_c_im, s5_glu_w)
    s5_selector = _s5_selector(s5_a_re.shape[-1])

    mem_n = _rmsnorm(mem.reshape(batch * n_mem, d), norm_mem)
    h = x.reshape(M, d)
    hn = _rmsnorm(h, norm_mix[0])
    for l in range(depth):
        i = l // 2
        if l % 2 == 0:
            ya = _mixer_a(hn, ev_w_in, i, ev_conv_w, seq_len)
            u = _mm(hn, ev_w_in, i, col0=3 * a_width, out_dtype=F32, name="s5_in")
            ys = _s5(u, s5_tables, s5_selector, s5_d, i, batch, seq_len)
            mixed, w_mix = [ya, ys], ev_w_out
        else:
            z = _mm(hn, od_w_qkv, i, bias=od_b_qkv, tn=1280, name="qkv")
            mixed, w_mix = [_swa(z, _swa_table(rel_bias, od_sinks[i]), batch, seq_len)], od_w_out
        kv = _mm(mem_n, xa_w_kv, l, name="xa_kv")
        h, hn = _out_xattn(mixed, w_mix, i, h, norm_xattn[l], kv, xa_w_q, xa_w_o, l, norm_ffn[l],
                           batch, seq_len, n_mem)
        act, w_down = _ffn_up(hn, ff_w_gate, ff_w_up, ff_w_down, l, ff_conv_w, ff_conv_b, seq_len)
        if l + 1 < depth:
            h, hn = _mm_res_norm([act], w_down, 0, h, norm_mix[l + 1], name="ff_down")
        else:
            _, out = _mm_res_norm([act], w_down, 0, h, norm_final, emit_h=False, norm_dtype=F32,
                                  name="ff_down_final")
    return out.reshape(batch, seq_len, d)
```
